```python
import math
import jax, jax.numpy as jnp
from jax import lax
import numpy as np

D_MODEL = 1024
BATCH = 16
SEQ = 2048
DEPTH = 2
DEC_BATCH = 32
DEC_SEQ = 8
PAST_LEN = 16384
PAGE_SIZE = 128

N_META = 16
MIX_WIDTH = D_MODEL
GROUP_WIDTH = MIX_WIDTH // 4
A_HEADS = 4
A_HEAD_DIM = GROUP_WIDTH // A_HEADS
FORGET_BIAS_MEAN = 3.0
B_HEADS = 4
B_V_DIM = GROUP_WIDTH // B_HEADS
B_QK_DIM = B_V_DIM // 2
C_HEADS = 4
C_HEAD_DIM = GROUP_WIDTH // C_HEADS
C_CHUNK = 64
C_CONV_CH = 3 * GROUP_WIDTH
D_HEADS = 4
D_HEAD_DIM = GROUP_WIDTH // D_HEADS
D_GROUPS = 2
D_STATE = 64
D_CHUNK = 128
D_CONV_CH = GROUP_WIDTH + 2 * D_GROUPS * D_STATE
CONV_W = 4
A_COLS = 3 * GROUP_WIDTH + A_HEADS
B_COLS = 3 * GROUP_WIDTH
C_COLS = C_CONV_CH + GROUP_WIDTH + 2 * C_HEADS
D_COLS = GROUP_WIDTH + D_CONV_CH + D_HEADS
IN_COLS = A_COLS + B_COLS + C_COLS + D_COLS
D_FF = 4 * D_MODEL
QBLOCK = 128
EPS = 1e-6

kernel_name = 'hybrid_fox_diff_gdn_ssd_decode_step'


def rmsnorm(x, g):
    xf = x.astype(jnp.float32)
    y = xf * lax.rsqrt(jnp.mean(xf * xf, axis=-1, keepdims=True) + EPS)
    return (y * g.astype(jnp.float32)).astype(x.dtype)


def l2norm(x):
    return x * lax.rsqrt(jnp.sum(x * x, axis=-1, keepdims=True) + EPS)


def split_cols(x, sizes):
    offs = np.cumsum([0] + list(sizes))
    return [x[..., int(offs[i]):int(offs[i + 1])] for i in range(len(sizes))]


def lambda_init(layer):
    return 0.8 - 0.6 * math.exp(-0.3 * layer)


def gather_pages(pool, page_table):
    g = jnp.take(pool, page_table, axis=0)
    return g.reshape((page_table.shape[0], page_table.shape[1] * pool.shape[1]) + pool.shape[2:])


def causal_conv(x, buf, w):
    xp = jnp.concatenate([buf.astype(x.dtype), x], axis=1)
    t = x.shape[1]
    y = sum(xp[:, i:i + t] * w[i] for i in range(CONV_W))
    return y, xp[:, xp.shape[1] - (CONV_W - 1):]


def attend(q, k, v, q_pos, k_pos, cq, ck):
    s = jnp.einsum('bqhd,bkhd->bhqk', q, k).astype(jnp.float32) * (q.shape[-1] ** -0.5)
    if cq is not None:
        s = s + jnp.swapaxes(cq, 1, 2)[..., :, None] - jnp.swapaxes(ck, 1, 2)[..., None, :]
    s = jnp.where(q_pos[:, None] >= k_pos[None, :], s, -jnp.inf)
    p = jax.nn.softmax(s, axis=-1).astype(v.dtype)
    return jnp.einsum('bhqk,bkhd->bqhd', p, v)


def causal_attention(q, k, v, c, bounds):
    off = k.shape[1] - q.shape[1]
    pos = jnp.arange(k.shape[1])
    outs = []
    for lo, hi in zip(bounds[:-1], bounds[1:]):
        kh = off + hi
        outs.append(attend(q[:, lo:hi], k[:, :kh], v[:, :kh], pos[off + lo:kh], pos[:kh],
                           None if c is None else c[:, off + lo:kh],
                           None if c is None else c[:, :kh]))
    return jnp.concatenate(outs, axis=1)


def gdn_chunk(seqs, s0):
    q, k, v, g, beta = seqs
    L = q.shape[1]
    nv = v.shape[-1]
    gam = jnp.cumsum(g, axis=1)
    gam_h = jnp.swapaxes(gam, 1, 2)
    idx = jnp.arange(L)
    decay = jnp.exp(jnp.where(idx[:, None] >= idx[None, :],
                              gam_h[..., :, None] - gam_h[..., None, :], -jnp.inf))
    beta_h = jnp.swapaxes(beta, 1, 2)
    a_mat = jnp.where(idx[:, None] > idx[None, :],
                      jnp.einsum('bthk,bshk->bhts', k, k) * decay, 0.0) * beta_h[..., :, None]
    rhs = jnp.swapaxes(jnp.concatenate([v, k * jnp.exp(gam)[..., None]], axis=-1) * beta[..., None], 1, 2)
    sol = lax.linalg.triangular_solve(a_mat + jnp.eye(L, dtype=a_mat.dtype), rhs,
                                      left_side=True, lower=True, unit_diagonal=True)
    u, w = sol[..., :nv], sol[..., nv:]
    v_new = u - jnp.einsum('bhtk,bhkv->bhtv', w, s0)
    o = (jnp.einsum('bhts,bhsv->bthv', jnp.einsum('bthk,bshk->bhts', q, k) * decay, v_new)
         + jnp.einsum('bthk,bhkv->bthv', q, s0) * jnp.exp(gam)[..., None])
    g_last = gam[:, -1]
    k_dec = k * jnp.exp(g_last[:, None] - gam)[..., None]
    s_new = jnp.exp(g_last)[..., None, None] * s0 + jnp.einsum('bshk,bhsv->bhkv', k_dec, v_new)
    return o, s_new


def ssd_chunk(seqs, s0):
    xdt, a, b_h, c_h = seqs
    L = xdt.shape[1]
    cum = jnp.cumsum(a, axis=1)
    cum_h = jnp.swapaxes(cum, 1, 2)
    idx = jnp.arange(L)
    decay = jnp.exp(jnp.where(idx[:, None] >= idx[None, :],
                              cum_h[..., :, None] - cum_h[..., None, :], -jnp.inf))
    scores = jnp.einsum('bthn,bshn->bhts', c_h, b_h) * decay
    y = (jnp.einsum('bhts,bshp->bthp', scores, xdt)
         + jnp.einsum('bthn,bhpn->bthp', c_h, s0) * jnp.exp(cum)[..., None])
    c_last = cum[:, -1]
    s_new = (jnp.exp(c_last)[..., None, None] * s0
             + jnp.einsum('bshn,bshp->bhpn', b_h * jnp.exp(c_last[:, None] - cum)[..., None], xdt))
    return y, s_new


def run_chunks(chunk_fn, seqs, state, lead, chunk):
    if lead is None:
        return chunk_fn(seqs, state)
    y0, state = chunk_fn(tuple(a[:, :lead] for a in seqs), state)
    rest = [a[:, lead:] for a in seqs]
    n = rest[0].shape[1] // chunk
    xs = tuple(jnp.swapaxes(a.reshape((a.shape[0], n, chunk) + a.shape[2:]), 0, 1) for a in rest)

    def step(s, c):
        y, s_new = chunk_fn(c, s)
        return s_new, y

    state, ys = lax.scan(step, state, xs)
    ys = jnp.swapaxes(ys, 0, 1).reshape((y0.shape[0], n * chunk) + y0.shape[2:])
    return jnp.concatenate([y0, ys], axis=1), state


def mixing_sublayer(h, past, lead, bounds, lam_init, w_in, fox_forget_bias, diff_lambda, diff_norm,
                    gdn_conv_w, gdn_A_log, gdn_dt_bias, gdn_norm, ssd_conv_w, ssd_conv_b,
                    ssd_A_log, ssd_dt_bias, ssd_D, ssd_norm, w_out):
    (fox_k_past, fox_v_past, fox_logf_past, diff_k_past, diff_v_past,
     gdn_conv_buf, gdn_state, ssd_conv_buf, ssd_state) = past
    f32 = jnp.float32
    bsz, t = h.shape[0], h.shape[1]
    proj = h @ w_in
    cols_a, cols_b, cols_c, cols_d = split_cols(proj, (A_COLS, B_COLS, C_COLS, D_COLS))

    fq, fk, fv, ff = split_cols(cols_a, (GROUP_WIDTH, GROUP_WIDTH, GROUP_WIDTH, A_HEADS))
    fq = fq.reshape(bsz, t, A_HEADS, A_HEAD_DIM)
    fk = fk.reshape(bsz, t, A_HEADS, A_HEAD_DIM)
    fv = fv.reshape(bsz, t, A_HEADS, A_HEAD_DIM)
    logf = jax.nn.log_sigmoid((ff + fox_forget_bias).astype(f32))
    logf_all = jnp.concatenate([fox_logf_past.astype(f32), logf], axis=1)
    c_all = logf_all - lax.cumsum(logf_all, axis=1, reverse=True)
    out_a = causal_attention(fq, jnp.concatenate([fox_k_past.astype(fk.dtype), fk], axis=1),
                             jnp.concatenate([fox_v_past.astype(fv.dtype), fv], axis=1), c_all, bounds)
    out_a = out_a.reshape(bsz, t, GROUP_WIDTH)

    dq, dk, dv = split_cols(cols_b, (GROUP_WIDTH, GROUP_WIDTH, GROUP_WIDTH))
    dq = dq.reshape(bsz, t, B_HEADS, 2, B_QK_DIM)
    dk = dk.reshape(bsz, t, B_HEADS, 2, B_QK_DIM)
    dv = dv.reshape(bsz, t, B_HEADS, B_V_DIM)
    dk_all = jnp.concatenate([diff_k_past.astype(dk.dtype), dk], axis=1)
    dv_all = jnp.concatenate([diff_v_past.astype(dv.dtype), dv], axis=1)
    lam_p = diff_lambda.astype(f32)
    lam = jnp.exp(jnp.sum(lam_p[0] * lam_p[1])) - jnp.exp(jnp.sum(lam_p[2] * lam_p[3])) + lam_init
    o1 = causal_attention(dq[..., 0, :], dk_all[..., 0, :], dv_all, None, bounds)
    o2 = causal_attention(dq[..., 1, :], dk_all[..., 1, :], dv_all, None, bounds)
    od = rmsnorm(o1.astype(f32) - lam * o2.astype(f32), diff_norm) * (1.0 - lam_init)
    out_b = od.reshape(bsz, t, GROUP_WIDTH).astype(h.dtype)

    c_qkv, c_z, c_a, c_b = split_cols(cols_c, (C_CONV_CH, GROUP_WIDTH, C_HEADS, C_HEADS))
    c_pre, gdn_conv_new = causal_conv(c_qkv, gdn_conv_buf, gdn_conv_w)
    cq, ck, cv = split_cols(jax.nn.silu(c_pre).astype(f32), (GROUP_WIDTH, GROUP_WIDTH, GROUP_WIDTH))
    cq = l2norm(cq.reshape(bsz, t, C_HEADS, C_HEAD_DIM)) * (C_HEAD_DIM ** -0.5)
    ck = l2norm(ck.reshape(bsz, t, C_HEADS, C_HEAD_DIM))
    cv = cv.reshape(bsz, t, C_HEADS, C_HEAD_DIM)
    g = -jnp.exp(gdn_A_log.astype(f32)) * jax.nn.softplus((c_a + gdn_dt_bias).astype(f32))
    beta = jax.nn.sigmoid(c_b.astype(f32))
    oc, gdn_state_new = run_chunks(gdn_chunk, (cq, ck, cv, g, beta), gdn_state.astype(f32), lead, C_CHUNK)
    oc = rmsnorm(oc, gdn_norm) * jax.nn.silu(c_z.astype(f32).reshape(bsz, t, C_HEADS, C_HEAD_DIM))
    out_c = oc.reshape(bsz, t, GROUP_WIDTH).astype(h.dtype)

    s_z, s_xbc, s_dt = split_cols(cols_d, (GROUP_WIDTH, D_CONV_CH, D_HEADS))
    s_pre, ssd_conv_new = causal_conv(s_xbc, ssd_conv_buf, ssd_conv_w)
    s_x, s_b, s_c = split_cols(jax.nn.silu(s_pre + ssd_conv_b).astype(f32),
                               (GROUP_WIDTH, D_GROUPS * D_STATE, D_GROUPS * D_STATE))
    rep = D_HEADS // D_GROUPS
    s_x = s_x.reshape(bsz, t, D_HEADS, D_HEAD_DIM)
    s_b = jnp.repeat(s_b.reshape(bsz, t, D_GROUPS, D_STATE), rep, axis=2)
    s_c = jnp.repeat(s_c.reshape(bsz, t, D_GROUPS, D_STATE), rep, axis=2)
    dt = jax.nn.softplus((s_dt + ssd_dt_bias).astype(f32))
    a = dt * -jnp.exp(ssd_A_log.astype(f32))
    ys, ssd_state_new = run_chunks(ssd_chunk, (s_x * dt[..., None], a, s_b, s_c),
                                   ssd_state.astype(f32), lead, D_CHUNK)
    ys = ys + ssd_D.astype(f32)[:, None] * s_x
    out_d = rmsnorm(ys.reshape(bsz, t, GROUP_WIDTH) * jax.nn.silu(s_z.astype(f32)), ssd_norm).astype(h.dtype)

    mix = jnp.concatenate([out_a, out_b, out_c, out_d], axis=-1) @ w_out
    new_rows = (fk, fv, logf, dk, dv, gdn_conv_new, gdn_state_new, ssd_conv_new, ssd_state_new)
    return mix, new_rows


def trunk(x, pasts, lead, bounds, g_mix_pre, w_in, fox_forget_bias, diff_lambda, diff_norm,
          gdn_conv_w, gdn_A_log, gdn_dt_bias, gdn_norm, ssd_conv_w, ssd_conv_b, ssd_A_log,
          ssd_dt_bias, ssd_D, ssd_norm, w_out, g_mix_post, g_mlp_pre, w_mlp_up, w_mlp_down, g_mlp_post):
    layer_rows = []
    for l in range(DEPTH):
        h = rmsnorm(x, g_mix_pre[l])
        mix, rows = mixing_sublayer(h, pasts[l], lead, bounds, lambda_init(l), w_in[l], fox_forget_bias[l],
                                    diff_lambda[l], diff_norm[l], gdn_conv_w[l], gdn_A_log[l],
                                    gdn_dt_bias[l], gdn_norm[l], ssd_conv_w[l], ssd_conv_b[l],
                                    ssd_A_log[l], ssd_dt_bias[l], ssd_D[l], ssd_norm[l], w_out[l])
        x = x + rmsnorm(mix, g_mix_post[l])
        h = rmsnorm(x, g_mlp_pre[l])
        u = jnp.square(jax.nn.relu(h @ w_mlp_up[l]))
        x = x + rmsnorm(u @ w_mlp_down[l], g_mlp_post[l])
        layer_rows.append(rows)
    stacked = tuple(jnp.stack([rows[i] for rows in layer_rows]) for i in range(len(layer_rows[0])))
    return x, stacked


def setup_inputs(seed: int = 0) -> dict:
    key = jax.random.key(seed)
    ks = jax.random.split(key, 40)
    f32 = jnp.float32
    n_pages = PAST_LEN // PAGE_SIZE
    n_used = DEC_BATCH * n_pages
    n_pool = n_used + n_used // 4

    def nrm(k, shape, scale=1.0):
        return scale * jax.random.normal(k, shape, f32)

    def gain(k, shape):
        return 1.0 + 0.02 * jax.random.normal(k, shape, f32)

    def a_log(k, shape):
        return jnp.log(jax.random.uniform(k, shape, f32, 1.0, 16.0))

    def dt_bias(k, shape):
        dt = jnp.exp(jax.random.uniform(k, shape, f32, math.log(1e-3), math.log(1e-1)))
        return dt + jnp.log(-jnp.expm1(-dt))

    page_table = jax.random.permutation(ks[0], n_pool)[:n_used].reshape(DEC_BATCH, n_pages).astype(jnp.int32)
    return {
        'x_prompt': nrm(ks[1], (BATCH, SEQ, D_MODEL)),
        'x_sample': nrm(ks[2], (DEC_BATCH, DEC_SEQ, D_MODEL)),
        'cache_fox_k': nrm(ks[3], (DEPTH, n_pool, PAGE_SIZE, A_HEADS, A_HEAD_DIM)),
        'cache_fox_v': nrm(ks[4], (DEPTH, n_pool, PAGE_SIZE, A_HEADS, A_HEAD_DIM)),
        'cache_fox_logf': jax.nn.log_sigmoid(FORGET_BIAS_MEAN + nrm(ks[5], (DEPTH, n_pool, PAGE_SIZE, A_HEADS))),
        'cache_diff_k': nrm(ks[6], (DEPTH, n_pool, PAGE_SIZE, B_HEADS, 2, B_QK_DIM)),
        'cache_diff_v': nrm(ks[7], (DEPTH, n_pool, PAGE_SIZE, B_HEADS, B_V_DIM)),
        'state_gdn_conv': nrm(ks[8], (DEPTH, DEC_BATCH, CONV_W - 1, C_CONV_CH)),
        'state_gdn': nrm(ks[9], (DEPTH, DEC_BATCH, C_HEADS, C_HEAD_DIM, C_HEAD_DIM), 0.3),
        'state_ssd_conv': nrm(ks[10], (DEPTH, DEC_BATCH, CONV_W - 1, D_CONV_CH)),
        'state_ssd': nrm(ks[11], (DEPTH, DEC_BATCH, D_HEADS, D_HEAD_DIM, D_STATE), 0.3),
        'page_table': page_table,
        'meta_tokens': nrm(ks[12], (N_META, D_MODEL)),
        'g_mix_pre': gain(ks[13], (DEPTH, D_MODEL)),
        'w_in': nrm(ks[14], (DEPTH, D_MODEL, IN_COLS), D_MODEL ** -0.5),
        'fox_forget_bias': FORGET_BIAS_MEAN + nrm(ks[15], (DEPTH, A_HEADS), 0.5),
        'diff_lambda': nrm(ks[16], (DEPTH, 4, B_QK_DIM), 0.1),
        'diff_norm': gain(ks[17], (DEPTH, B_V_DIM)),
        'gdn_conv_w': nrm(ks[18], (DEPTH, CONV_W, C_CONV_CH), CONV_W ** -0.5),
        'gdn_A_log': a_log(ks[19], (DEPTH, C_HEADS)),
        'gdn_dt_bias': dt_bias(ks[20], (DEPTH, C_HEADS)),
        'gdn_norm': gain(ks[21], (DEPTH, C_HEAD_DIM)),
        'ssd_conv_w': nrm(ks[22], (DEPTH, CONV_W, D_CONV_CH), CONV_W ** -0.5),
        'ssd_conv_b': nrm(ks[23], (DEPTH, D_CONV_CH), 0.02),
        'ssd_A_log': a_log(ks[24], (DEPTH, D_HEADS)),
        'ssd_dt_bias': dt_bias(ks[25], (DEPTH, D_HEADS)),
        'ssd_D': 1.0 + nrm(ks[26], (DEPTH, D_HEADS), 0.1),
        'ssd_norm': gain(ks[27], (DEPTH, GROUP_WIDTH)),
        'w_out': nrm(ks[28], (DEPTH, MIX_WIDTH, D_MODEL), MIX_WIDTH ** -0.5),
        'g_mix_post': gain(ks[29], (DEPTH, D_MODEL)),
        'g_mlp_pre': gain(ks[30], (DEPTH, D_MODEL)),
        'w_mlp_up': nrm(ks[31], (DEPTH, D_MODEL, D_FF), D_MODEL ** -0.5),
        'w_mlp_down': nrm(ks[32], (DEPTH, D_FF, D_MODEL), D_FF ** -0.5),
        'g_mlp_post': gain(ks[33], (DEPTH, D_MODEL)),
    }


def reference(x_prompt, x_sample, cache_fox_k, cache_fox_v, cache_fox_logf, cache_diff_k, cache_diff_v,
              state_gdn_conv, state_gdn, state_ssd_conv, state_ssd, page_table, meta_tokens,
              g_mix_pre, w_in, fox_forget_bias, diff_lambda, diff_norm, gdn_conv_w, gdn_A_log,
              gdn_dt_bias, gdn_norm, ssd_conv_w, ssd_conv_b, ssd_A_log, ssd_dt_bias, ssd_D, ssd_norm,
              w_out, g_mix_post, g_mlp_pre, w_mlp_up, w_mlp_down, g_mlp_post):
    weights = (g_mix_pre, w_in, fox_forget_bias, diff_lambda, diff_norm, gdn_conv_w, gdn_A_log,
               gdn_dt_bias, gdn_norm, ssd_conv_w, ssd_conv_b, ssd_A_log, ssd_dt_bias, ssd_D, ssd_norm,
               w_out, g_mix_post, g_mlp_pre, w_mlp_up, w_mlp_down, g_mlp_post)
    f32 = jnp.float32
    dt = x_prompt.dtype

    bsz = x_prompt.shape[0]
    xp = jnp.concatenate([jnp.broadcast_to(meta_tokens.astype(dt)[None], (bsz, N_META, D_MODEL)), x_prompt], axis=1)
    t_p = xp.shape[1]
    bounds_p = [0] + list(range(N_META, t_p, QBLOCK)) + [t_p]
    prompt_past = (jnp.zeros((bsz, 0, A_HEADS, A_HEAD_DIM), dt), jnp.zeros((bsz, 0, A_HEADS, A_HEAD_DIM), dt),
                   jnp.zeros((bsz, 0, A_HEADS), f32), jnp.zeros((bsz, 0, B_HEADS, 2, B_QK_DIM), dt),
                   jnp.zeros((bsz, 0, B_HEADS, B_V_DIM), dt), jnp.zeros((bsz, CONV_W - 1, C_CONV_CH), dt),
                   jnp.zeros((bsz, C_HEADS, C_HEAD_DIM, C_HEAD_DIM), f32),
                   jnp.zeros((bsz, CONV_W - 1, D_CONV_CH), dt),
                   jnp.zeros((bsz, D_HEADS, D_HEAD_DIM, D_STATE), f32))
    y_p, new_p = trunk(xp, [prompt_past] * DEPTH, N_META, bounds_p, *weights)
    y_prompt = y_p[:, N_META:]
    (p_fox_k, p_fox_v, p_fox_logf, p_diff_k, p_diff_v,
     p_gdn_conv, p_gdn, p_ssd_conv, p_ssd) = new_p

    sample_pasts = [(gather_pages(cache_fox_k[l], page_table), gather_pages(cache_fox_v[l], page_table),
                     gather_pages(cache_fox_logf[l], page_table), gather_pages(cache_diff_k[l], page_table),
                     gather_pages(cache_diff_v[l], page_table), state_gdn_conv[l], state_gdn[l],
                     state_ssd_conv[l], state_ssd[l]) for l in range(DEPTH)]
    y_sample, new_s = trunk(x_sample, sample_pasts, None, [0, x_sample.shape[1]], *weights)
    (s_fox_k, s_fox_v, s_fox_logf, s_diff_k, s_diff_v,
     s_gdn_conv, s_gdn, s_ssd_conv, s_ssd) = new_s

    return (y_prompt, y_sample,
            p_fox_k, p_fox_v, p_fox_logf, p_diff_k, p_diff_v, p_gdn_conv, p_gdn, p_ssd_conv, p_ssd,
            s_fox_k, s_fox_v, s_fox_logf, s_diff_k, s_diff_v, s_gdn_conv, s_gdn, s_ssd_conv, s_ssd)
```

```python
import functools
import math

import jax
import jax.numpy as jnp
from jax import lax
from jax.experimental import pallas as pl
from jax.experimental.pallas import tpu as pltpu

F32 = jnp.float32
BF16 = jnp.bfloat16
HI = lax.Precision.HIGHEST

D_MODEL = 1024
DEPTH = 2
GROUP_WIDTH = 256
N_HEADS = 4
HEAD_DIM = 64
DIFF_QK = 32
D_STATE = 64
CONV_W = 4
C_CONV_CH = 3 * GROUP_WIDTH
D_CONV_CH = GROUP_WIDTH + 2 * 2 * D_STATE
A_COLS = 3 * GROUP_WIDTH + N_HEADS
B_COLS = 3 * GROUP_WIDTH
C_COLS = C_CONV_CH + GROUP_WIDTH + 2 * N_HEADS
D_FF = 4 * D_MODEL
PAGE_SIZE = 128
EPS = 1e-6
NEG = -1e30

LANES = 128
SUBLANES = 8
SMALL_W = 128
SMALL_ROWS = 16
ATTN_TILE = 128
GDN_CHUNK = 64
SSD_CHUNK = 128
PAGES_PER_STEP = 8
SUFFIX_PAGES = 32
VMEM_LIMIT = 56 * 1024 * 1024

PROJ_WIDTHS = (256, 256, 256, 256, 256, 256, C_CONV_CH, 256, 256, D_CONV_CH, SMALL_W)


def _softplus(x):
    return jnp.maximum(x, 0.0) + jnp.log1p(jnp.exp(-jnp.abs(x)))


def _sigmoid(x):
    return 1.0 / (1.0 + jnp.exp(-x))


def _silu(x):
    return x * _sigmoid(x)


def _dot(a, b, prec=None):
    return jnp.dot(a, b, preferred_element_type=F32, precision=prec)


def _dot_nt(a, b, prec=None):
    return lax.dot_general(a, b, (((1,), (1,)), ((), ())), preferred_element_type=F32, precision=prec)


def _dot_tn(a, b, prec=None):
    return lax.dot_general(a, b, (((0,), (0,)), ((), ())), preferred_element_type=F32, precision=prec)


def _iota(shape, dim):
    return lax.broadcasted_iota(jnp.int32, shape, dim)


def _lane_group_mask(width, group, idx):
    return (_iota((1, width), 1) // group) == idx


def _block_ones(n, group):
    return ((_iota((n, n), 0) // group) == (_iota((n, n), 1) // group)).astype(F32)


def _lower_tri(n, strict=False):
    ri, ci = _iota((n, n), 0), _iota((n, n), 1)
    return (ri > ci) if strict else (ri >= ci)


def _stack_masked(x, group, groups):
    zero = jnp.zeros_like(x)
    return jnp.concatenate([jnp.where(_lane_group_mask(x.shape[1], group, g), x, zero) for g in groups], axis=0)


def _unstack_sum(xs, rows, n_groups):
    out = xs[0:rows]
    for g in range(1, n_groups):
        out = out + xs[g * rows:(g + 1) * rows]
    return out


def _unstack_select(xs, rows, group, blocks):
    out = None
    for g, blk in enumerate(blocks):
        part = jnp.where(_lane_group_mask(xs.shape[1], group, g), xs[blk * rows:(blk + 1) * rows], 0.0)
        out = part if out is None else out + part
    return out


def _pad_rows(x, rows):
    if x.shape[0] == rows:
        return x
    return jnp.concatenate([x, jnp.zeros((rows - x.shape[0], x.shape[1]), x.dtype)], axis=0)


def _cparams(n_grid_dims):
    return pltpu.CompilerParams(dimension_semantics=("arbitrary",) * n_grid_dims,
                                vmem_limit_bytes=VMEM_LIMIT)


def _in_proj_body(x_ref, g_ref, w_ref, wst_ref, *out_refs):
    x = x_ref[...]
    ms = jnp.mean(x * x, axis=-1, keepdims=True)
    h = (x * lax.rsqrt(ms + EPS) * g_ref[...]).astype(BF16)
    off = 0
    for o_ref, width in zip(out_refs[:-1], PROJ_WIDTHS):
        o_ref[...] = _dot(h, w_ref[:, off:off + width])
        off += width
    out_refs[-1][...] = _dot_nt(wst_ref[...], h)


def _in_proj(x2d, g, w_main, w_small_t, tm):
    n, d = x2d.shape
    assert n % tm == 0
    out_shape = [jax.ShapeDtypeStruct((n, w), F32) for w in PROJ_WIDTHS]
    out_shape.append(jax.ShapeDtypeStruct((SMALL_ROWS, n), F32))
    out_specs = [pl.BlockSpec((tm, w), lambda i: (i, 0)) for w in PROJ_WIDTHS]
    out_specs.append(pl.BlockSpec((SMALL_ROWS, tm), lambda i: (0, i)))
    return pl.pallas_call(
        _in_proj_body,
        grid=(n // tm,),
        in_specs=[pl.BlockSpec((tm, d), lambda i: (i, 0)),
                  pl.BlockSpec((1, d), lambda i: (0, 0)),
                  pl.BlockSpec(w_main.shape, lambda i: (0, 0)),
                  pl.BlockSpec(w_small_t.shape, lambda i: (0, 0))],
        out_specs=out_specs,
        out_shape=out_shape,
        compiler_params=_cparams(1),
        name="in_proj",
    )(x2d, g.reshape(1, d), w_main, w_small_t)


def _post_body(x_ref, oa_ref, ob_ref, oc_ref, od_ref, wo_ref, gpost_ref, gpre_ref, wu_ref, wd_ref,
               gmlp_ref, y_ref, *, ff_chunk):
    mix = None
    for i, o_ref in enumerate((oa_ref, ob_ref, oc_ref, od_ref)):
        part = _dot(o_ref[...].astype(BF16), wo_ref[i * GROUP_WIDTH:(i + 1) * GROUP_WIDTH, :])
        mix = part if mix is None else mix + part
    ms = jnp.mean(mix * mix, axis=-1, keepdims=True)
    x1 = x_ref[...] + mix * lax.rsqrt(ms + EPS) * gpost_ref[...]
    ms = jnp.mean(x1 * x1, axis=-1, keepdims=True)
    h = (x1 * lax.rsqrt(ms + EPS) * gpre_ref[...]).astype(BF16)
    acc = None
    for c in range(D_FF // ff_chunk):
        u = jnp.maximum(_dot(h, wu_ref[:, c * ff_chunk:(c + 1) * ff_chunk]), 0.0)
        part = _dot((u * u).astype(BF16), wd_ref[c * ff_chunk:(c + 1) * ff_chunk, :])
        acc = part if acc is None else acc + part
    ms = jnp.mean(acc * acc, axis=-1, keepdims=True)
    y_ref[...] = x1 + acc * lax.rsqrt(ms + EPS) * gmlp_ref[...]


def _post(x2d, outs, w_out, g_post, g_pre, w_up, w_down, g_mlp, tm):
    n, d = x2d.shape
    assert n % tm == 0
    row = lambda i: (i, 0)
    const = lambda i: (0, 0)
    single = pl.Buffered(1)
    return pl.pallas_call(
        functools.partial(_post_body, ff_chunk=512),
        grid=(n // tm,),
        in_specs=[pl.BlockSpec((tm, d), row)]
        + [pl.BlockSpec((tm, GROUP_WIDTH), row)] * 4
        + [pl.BlockSpec(w_out.shape, const, pipeline_mode=single),
           pl.BlockSpec((1, d), const), pl.BlockSpec((1, d), const),
           pl.BlockSpec(w_up.shape, const, pipeline_mode=single),
           pl.BlockSpec(w_down.shape, const, pipeline_mode=single),
           pl.BlockSpec((1, d), const)],
        out_specs=pl.BlockSpec((tm, d), row),
        out_shape=jax.ShapeDtypeStruct((n, d), F32),
        compiler_params=_cparams(1),
        name="post_mlp",
    )(x2d, *outs, w_out, g_post.reshape(1, d), g_pre.reshape(1, d), w_up, w_down, g_mlp.reshape(1, d))


def _softmax_update(qs, k_bf, v_bf, bias, mask, m_s, l_s, acc_s):
    s = _dot_nt(qs, k_bf)
    if bias is not None:
        s = s + bias
    if mask is not None:
        s = jnp.where(mask, s, NEG)
    m_old = m_s[...]
    m_new = jnp.maximum(m_old, jnp.max(s, axis=-1, keepdims=True))
    alpha = jnp.exp(m_old - m_new)
    p = jnp.exp(s - m_new)
    l_s[...] = alpha * l_s[...] + jnp.sum(p, axis=-1, keepdims=True)
    acc_s[...] = alpha * acc_s[...] + _dot(p.astype(BF16), v_bf)
    m_s[...] = m_new


def _diff_combine(o1, o2, lam_ref, norm_ref, lam_init):
    lp = lam_ref[...]
    lam = (jnp.exp(jnp.sum(lp[0:1] * lp[1:2], axis=-1, keepdims=True))
           - jnp.exp(jnp.sum(lp[2:3] * lp[3:4], axis=-1, keepdims=True)) + lam_init)
    od = o1 - lam * o2
    ms = _dot(od * od, _block_ones(GROUP_WIDTH, HEAD_DIM), HI) * (1.0 / HEAD_DIM)
    return od * lax.rsqrt(ms + EPS) * norm_ref[...] * (1.0 - lam_init)


def _stack_rows(x, size):
    return jnp.concatenate([x[:, h:h + 1] for h in range(N_HEADS)], axis=0)


def _stack_lanes(x, size):
    return jnp.concatenate([jnp.broadcast_to(x[h:h + 1, :], (size, x.shape[1])) for h in range(N_HEADS)],
                           axis=0)


def _prompt_attn_body(*refs, fox, seq, lam_init):
    if fox:
        (q_ref, k_ref, v_ref, sm_ref, smt_ref, prow_ref, pcol_ref,
         o_ref, logf_ref, qb_s, kb_s, vb_s, m_s, l_s, acc_s, ccol_s, crow_s) = refs
    else:
        (q_ref, k_ref, v_ref, lam_ref, norm_ref, o_ref, qb_s, kb_s, vb_s, m_s, l_s, acc_s) = refs
    t = ATTN_TILE
    n_tiles = -(-seq // t)
    seq_pad = n_tiles * t
    last_valid = seq - (n_tiles - 1) * t
    group = HEAD_DIM if fox else DIFF_QK

    qb_s[0:seq, :] = (q_ref[0] * (group ** -0.5)).astype(BF16)
    kb_s[0:seq, :] = k_ref[0].astype(BF16)
    vb_s[0:seq, :] = v_ref[0].astype(BF16)
    if seq_pad > seq:
        zeros = jnp.zeros((seq_pad - seq, GROUP_WIDTH), BF16)
        qb_s[seq:seq_pad, :] = zeros
        kb_s[seq:seq_pad, :] = zeros
        vb_s[seq:seq_pad, :] = zeros

    if fox:
        bias_row = prow_ref[0:1, :]
        bias_col = pcol_ref[0:SUBLANES, 0:1]
        tri_lo = _lower_tri(t).astype(F32)
        tri_up = (_iota((t, t), 0) <= _iota((t, t), 1)).astype(F32)

        def cum_tile(start, valid, carry):
            c_col, c_row = carry
            lf = -_softplus(-(_pad_rows(sm_ref[0, pl.ds(start, valid), :], t) + bias_row))
            cc = _dot(tri_lo, lf, HI) + c_col
            ccol_s[pl.ds(start, t), :] = cc
            logf_ref[0, pl.ds(start, valid), :] = lf[0:valid, 0:N_HEADS]
            lft = -_softplus(-(smt_ref[0, 0:SUBLANES, pl.ds(start, t)] + bias_col))
            cr = _dot(lft, tri_up, HI) + c_row
            crow_s[:, pl.ds(start, t)] = cr
            return cc[t - 1:t, :], cr[:, t - 1:t]

        carry = (jnp.zeros((1, SMALL_W), F32), jnp.zeros((SUBLANES, 1), F32))
        carry = lax.fori_loop(0, n_tiles - 1, lambda i, c: cum_tile(pl.multiple_of(i * t, t), t, c), carry)
        cum_tile((n_tiles - 1) * t, last_valid, carry)

    causal = jnp.concatenate([_lower_tri(t)] * N_HEADS, axis=0)

    def q_tile(idx, valid):
        qs0 = idx * t if isinstance(idx, int) else pl.multiple_of(idx * t, t)
        q = qb_s[pl.ds(qs0, t), :]

        def run(first):
            step = HEAD_DIM // group
            qstack = _stack_masked(q, group, [first + step * h for h in range(N_HEADS)])
            if fox:
                cq = _stack_rows(ccol_s[pl.ds(qs0, t), :], t)
            m_s[...] = jnp.full(m_s.shape, NEG, F32)
            l_s[...] = jnp.zeros(l_s.shape, F32)
            acc_s[...] = jnp.zeros(acc_s.shape, F32)

            def key_tile(ks, mask):
                bias = (cq - _stack_lanes(crow_s[:, pl.ds(ks, t)], t)) if fox else None
                _softmax_update(qstack, kb_s[pl.ds(ks, t), :], vb_s[pl.ds(ks, t), :], bias, mask,
                                m_s, l_s, acc_s)

            def full_step(j, c):
                key_tile(pl.multiple_of(j * t, t), None)
                return c

            lax.fori_loop(0, idx, full_step, 0)
            key_tile(qs0, causal)
            return _unstack_select(acc_s[...] / l_s[...], t, HEAD_DIM, range(N_HEADS))

        if fox:
            o = run(0)
        else:
            o = _diff_combine(run(0), run(1), lam_ref, norm_ref, lam_init)
        o_ref[0, pl.ds(qs0, valid), :] = o[0:valid]

    def loop_body(i, c):
        q_tile(i, t)
        return c

    lax.fori_loop(0, n_tiles - 1, loop_body, 0)
    q_tile(n_tiles - 1, last_valid)


def _prompt_attention(q, k, v, *, fox, extras, lam_init=0.0):
    b, seq, w = q.shape
    t = ATTN_TILE
    seq_pad = -(-seq // t) * t
    seq_spec = pl.BlockSpec((1, seq, w), lambda i: (i, 0, 0))
    const2 = lambda i: (0, 0)
    scratch = [pltpu.VMEM((seq_pad, w), BF16)] * 3 + [
        pltpu.VMEM((N_HEADS * t, 1), F32), pltpu.VMEM((N_HEADS * t, 1), F32),
        pltpu.VMEM((N_HEADS * t, GROUP_WIDTH), F32)]
    if fox:
        sm, smt, prow, pcol = extras
        assert smt.shape[2] == seq_pad
        in_specs = [seq_spec] * 3 + [pl.BlockSpec((1, seq, SMALL_W), lambda i: (i, 0, 0)),
                                     pl.BlockSpec((1, SMALL_ROWS, seq_pad), lambda i: (i, 0, 0)),
                                     pl.BlockSpec(prow.shape, const2), pl.BlockSpec(pcol.shape, const2)]
        out_shape = [jax.ShapeDtypeStruct((b, seq, w), F32), jax.ShapeDtypeStruct((b, seq, N_HEADS), F32)]
        out_specs = [seq_spec, pl.BlockSpec((1, seq, N_HEADS), lambda i: (i, 0, 0))]
        scratch = scratch + [pltpu.VMEM((seq_pad, SMALL_W), F32), pltpu.VMEM((SUBLANES, seq_pad), F32)]
    else:
        lam, norm = extras
        in_specs = [seq_spec] * 3 + [pl.BlockSpec(lam.shape, const2), pl.BlockSpec(norm.shape, const2)]
        out_shape = jax.ShapeDtypeStruct((b, seq, w), F32)
        out_specs = seq_spec
    return pl.pallas_call(
        functools.partial(_prompt_attn_body, fox=fox, seq=seq, lam_init=lam_init),
        grid=(b,),
        in_specs=in_specs,
        out_specs=out_specs,
        out_shape=out_shape,
        scratch_shapes=scratch,
        compiler_params=_cparams(1),
        name="prompt_fox" if fox else "prompt_diff",
    )(q, k, v, *extras)


def _fill_conv_scratch(x_ref, buf_ref, xpad_s, new_ref, seq):
    xpad_s[0:SUBLANES, :] = buf_ref[0]
    xpad_s[SUBLANES:SUBLANES + seq, :] = x_ref[0]
    tail = xpad_s.shape[0] - SUBLANES - seq
    if tail:
        xpad_s[SUBLANES + seq:, :] = jnp.zeros((tail, xpad_s.shape[1]), F32)
    new_ref[0] = xpad_s[seq:seq + SUBLANES, :]


def _conv_chunk(xpad_s, w_ref, start, size):
    n = size + SUBLANES
    win = xpad_s[pl.ds(start, n), :]
    out = None
    for i in range(CONV_W):
        first = SUBLANES - (CONV_W - 1) + i
        part = pltpu.roll(win, n - first, axis=0)[0:size] * w_ref[i:i + 1, :]
        out = part if out is None else out + part
    return out


def _expand_heads(sm, first_lane):
    sel = (_iota((SMALL_W, GROUP_WIDTH), 0) == first_lane + _iota((SMALL_W, GROUP_WIDTH), 1) // HEAD_DIM)
    return _dot(sm, sel.astype(F32), HI)


def _head_rows(x_exp):
    sel = (_iota((SUBLANES, GROUP_WIDTH), 0) == _iota((SUBLANES, GROUP_WIDTH), 1) // HEAD_DIM)
    return _dot_nt(sel.astype(F32) * (1.0 / HEAD_DIM), x_exp, HI)


def _inv_unit_lower(a, n, blk):
    ri = _iota((n, n), 0)
    ci = _iota((n, n), 1)
    eye = (ri == ci).astype(F32)
    base = min(16, blk)
    bneg = -jnp.where((ri // base) == (ci // base), a, 0.0)
    x = eye + bneg
    p = bneg
    s = 1
    while 2 * s < base:
        p = _dot(p, p, HI)
        x = x + _dot(x, p, HI)
        s *= 2
    size = base
    while size < blk:
        c = jnp.where(((ri // (2 * size)) == (ci // (2 * size))) & ((ri // size) != (ci // size)), a, 0.0)
        x = x - _dot(_dot(x, c, HI), x, HI)
        size *= 2
    return x


def _chunk_schedule(seq, chunk, fn):
    n_chunks = -(-seq // chunk)
    if n_chunks > 1:
        def body(i, c):
            fn(pl.multiple_of(i * chunk, chunk), chunk)
            return c
        lax.fori_loop(0, n_chunks - 1, body, 0)
    fn((n_chunks - 1) * chunk, seq - (n_chunks - 1) * chunk)


def _gdn_body(x_ref, z_ref, sm_ref, buf_ref, s0_ref, w_ref, par_ref,
              o_ref, new_ref, sn_ref, xpad_s, st_s, *, seq):
    _fill_conv_scratch(x_ref, buf_ref, xpad_s, new_ref, seq)
    st_s[...] = s0_ref[0]
    size = GDN_CHUNK
    n = N_HEADS * size
    dtb = par_ref[0:1, :]
    a_neg = -jnp.exp(par_ref[1:2, :])
    gain = par_ref[2:3, :]
    head_ones = _block_ones(GROUP_WIDTH, HEAD_DIM)
    heads = range(N_HEADS)

    def chunk(start, valid):
        pre = _silu(_conv_chunk(xpad_s, w_ref, start, size))
        q = pre[:, 0:GROUP_WIDTH]
        k = pre[:, GROUP_WIDTH:2 * GROUP_WIDTH]
        v = pre[:, 2 * GROUP_WIDTH:3 * GROUP_WIDTH]
        q = q * lax.rsqrt(_dot(q * q, head_ones, HI) + EPS) * (HEAD_DIM ** -0.5)
        k = k * lax.rsqrt(_dot(k * k, head_ones, HI) + EPS)
        sm = _pad_rows(sm_ref[0, pl.ds(start, valid), :], size)
        g = a_neg * _softplus(_expand_heads(sm, 4) + dtb)
        beta = _sigmoid(_expand_heads(sm, 8))
        if valid < size:
            live = _iota((size, 1), 0) < valid
            g = jnp.where(live, g, 0.0)
            beta = jnp.where(live, beta, 0.0)
        gam = _dot(_lower_tri(size).astype(F32), g, HI)
        egam = jnp.exp(gam)
        gam_stack = _stack_masked(gam, HEAD_DIM, heads)
        gcol = jnp.concatenate([gam[:, h * HEAD_DIM:h * HEAD_DIM + 1] for h in heads], axis=0)
        grow = _dot_nt(jnp.full((SUBLANES, GROUP_WIDTH), 1.0 / HEAD_DIM, F32), gam_stack, HI)[0:1]
        bcol = jnp.concatenate([beta[:, h * HEAD_DIM:h * HEAD_DIM + 1] for h in heads], axis=0)
        ri = _iota((n, n), 0)
        ci = _iota((n, n), 1)
        same = (ri // size) == (ci // size)
        decay = jnp.exp(jnp.where(same & (ri >= ci), gcol - grow, -jnp.inf))
        ks = _stack_masked(k, HEAD_DIM, heads)
        qs = _stack_masked(q, HEAD_DIM, heads)
        a_mat = jnp.where(same & (ri > ci), _dot_nt(ks, ks, HI) * decay, 0.0) * bcol
        t_inv = _inv_unit_lower(a_mat, n, size)
        rhs = jnp.concatenate([_stack_masked(v * beta, HEAD_DIM, heads),
                               _stack_masked(k * egam * beta, HEAD_DIM, heads)], axis=1)
        uw = _dot(t_inv, rhs, HI)
        u = _unstack_sum(uw[:, 0:GROUP_WIDTH], size, N_HEADS)
        w = _unstack_sum(uw[:, GROUP_WIDTH:], size, N_HEADS)
        state = st_s[...]
        v_new = u - _dot(w, state, HI)
        pm = _dot_nt(qs, ks, HI) * decay
        o = (_unstack_sum(_dot(pm, _stack_masked(v_new, HEAD_DIM, heads), HI), size, N_HEADS)
             + _dot(q * egam, state, HI))
        g_last = gam[size - 1:size, :]
        k_dec = k * jnp.exp(g_last - gam)
        st_s[...] = state * jnp.exp(g_last) + _dot_tn(k_dec, v_new, HI) * head_ones
        ms = _dot(o * o, head_ones, HI) * (1.0 / HEAD_DIM)
        y = o * lax.rsqrt(ms + EPS) * gain
        o_ref[0, pl.ds(start, valid), :] = y[0:valid] * _silu(z_ref[0, pl.ds(start, valid), :])

    _chunk_schedule(seq, size, chunk)
    sn_ref[0] = st_s[...]


def _gdn(x, z, sm, buf8, s0, conv_w8, par):
    b, seq, ch = x.shape
    seq_pad = -(-seq // GDN_CHUNK) * GDN_CHUNK
    bspec = lambda shape: pl.BlockSpec((1,) + shape, lambda i: (i, 0, 0))
    const2 = lambda i: (0, 0)
    return pl.pallas_call(
        functools.partial(_gdn_body, seq=seq),
        grid=(b,),
        in_specs=[bspec((seq, ch)), bspec((seq, GROUP_WIDTH)), bspec((seq, SMALL_W)), bspec((SUBLANES, ch)),
                  bspec((GROUP_WIDTH, GROUP_WIDTH)), pl.BlockSpec(conv_w8.shape, const2),
                  pl.BlockSpec(par.shape, const2)],
        out_specs=[bspec((seq, GROUP_WIDTH)), bspec((SUBLANES, ch)), bspec((GROUP_WIDTH, GROUP_WIDTH))],
        out_shape=[jax.ShapeDtypeStruct((b, seq, GROUP_WIDTH), F32),
                   jax.ShapeDtypeStruct((b, SUBLANES, ch), F32),
                   jax.ShapeDtypeStruct((b, GROUP_WIDTH, GROUP_WIDTH), F32)],
        scratch_shapes=[pltpu.VMEM((seq_pad + 2 * SUBLANES, ch), F32),
                        pltpu.VMEM((GROUP_WIDTH, GROUP_WIDTH), F32)],
        compiler_params=_cparams(1),
        name="gdn",
    )(x, z, sm, buf8, s0, conv_w8, par)


def _ssd_body(x_ref, z_ref, sm_ref, buf_ref, s0_ref, w_ref, cb_ref, par_ref,
              o_ref, new_ref, sn_ref, xpad_s, st_s, *, seq):
    _fill_conv_scratch(x_ref, buf_ref, xpad_s, new_ref, seq)
    st_s[...] = s0_ref[0]
    size = SSD_CHUNK
    dtb = par_ref[0:1, :]
    a_neg = -jnp.exp(par_ref[1:2, :])
    d_skip = par_ref[2:3, :]
    gain = par_ref[3:4, :]
    n_state = 2 * D_STATE
    live_state = ((_iota((n_state, GROUP_WIDTH), 0) // D_STATE)
                  == (_iota((n_state, GROUP_WIDTH), 1) // (2 * HEAD_DIM))).astype(F32)

    def chunk(start, valid):
        pre = _silu(_conv_chunk(xpad_s, w_ref, start, size) + cb_ref[...])
        sx = pre[:, 0:GROUP_WIDTH]
        sb = pre[:, GROUP_WIDTH:GROUP_WIDTH + n_state]
        sc = pre[:, GROUP_WIDTH + n_state:]
        sm = _pad_rows(sm_ref[0, pl.ds(start, valid), :], size)
        dt = _softplus(_expand_heads(sm, 12) + dtb)
        if valid < size:
            dt = jnp.where(_iota((size, 1), 0) < valid, dt, 0.0)
        a = dt * a_neg
        tri = _lower_tri(size)
        cum = _dot(tri.astype(F32), a, HI)
        ecum = jnp.exp(cum)
        cum_rows = _head_rows(cum)
        xdt = sx * dt
        sb_bf = sb.astype(BF16)
        cb = [_dot_nt(jnp.where(_lane_group_mask(n_state, D_STATE, g), sc, 0.0).astype(BF16), sb_bf)
              for g in range(2)]
        y = None
        for h in range(N_HEADS):
            dec = jnp.exp(jnp.where(tri, cum[:, h * HEAD_DIM:h * HEAD_DIM + 1] - cum_rows[h:h + 1, :], -jnp.inf))
            xh = jnp.where(_lane_group_mask(GROUP_WIDTH, HEAD_DIM, h), xdt, 0.0).astype(BF16)
            part = _dot((cb[h // 2] * dec).astype(BF16), xh)
            y = part if y is None else y + part
        state = st_s[...]
        y = y + _dot(sc.astype(BF16), state.astype(BF16)) * ecum
        c_last = cum[size - 1:size, :]
        st_s[...] = (state * jnp.exp(c_last)
                     + _dot_tn(sb_bf, (xdt * jnp.exp(c_last - cum)).astype(BF16)) * live_state)
        ys = (y + d_skip * sx)[0:valid] * _silu(z_ref[0, pl.ds(start, valid), :])
        ms = jnp.mean(ys * ys, axis=-1, keepdims=True)
        o_ref[0, pl.ds(start, valid), :] = ys * lax.rsqrt(ms + EPS) * gain

    _chunk_schedule(seq, size, chunk)
    sn_ref[0] = st_s[...]


def _ssd(x, z, sm, buf8, s0, conv_w8, conv_b, par):
    b, seq, ch = x.shape
    n_state = 2 * D_STATE
    seq_pad = -(-seq // SSD_CHUNK) * SSD_CHUNK
    bspec = lambda shape: pl.BlockSpec((1,) + shape, lambda i: (i, 0, 0))
    const2 = lambda i: (0, 0)
    return pl.pallas_call(
        functools.partial(_ssd_body, seq=seq),
        grid=(b,),
        in_specs=[bspec((seq, ch)), bspec((seq, GROUP_WIDTH)), bspec((seq, SMALL_W)), bspec((SUBLANES, ch)),
                  bspec((n_state, GROUP_WIDTH)), pl.BlockSpec(conv_w8.shape, const2),
                  pl.BlockSpec(conv_b.shape, const2), pl.BlockSpec(par.shape, const2)],
        out_specs=[bspec((seq, GROUP_WIDTH)), bspec((SUBLANES, ch)), bspec((n_state, GROUP_WIDTH))],
        out_shape=[jax.ShapeDtypeStruct((b, seq, GROUP_WIDTH), F32),
                   jax.ShapeDtypeStruct((b, SUBLANES, ch), F32),
                   jax.ShapeDtypeStruct((b, n_state, GROUP_WIDTH), F32)],
        scratch_shapes=[pltpu.VMEM((seq_pad + 2 * SUBLANES, ch), F32), pltpu.VMEM((n_state, GROUP_WIDTH), F32)],
        compiler_params=_cparams(1),
        name="ssd",
    )(x, z, sm, buf8, s0, conv_w8, conv_b, par)


def _suffix_body(x_ref, o_ref):
    x = x_ref[0]
    rows = x.shape[0]
    rows_per_page = PAGE_SIZE * N_HEADS // LANES
    li = _iota((LANES, LANES), 0)
    lj = _iota((LANES, LANES), 1)
    same_head = (li % N_HEADS) == (lj % N_HEADS)
    within = (same_head & ((li // N_HEADS) >= (lj // N_HEADS))).astype(F32)
    total = same_head.astype(F32)
    ri = _iota((rows, rows), 0)
    rj = _iota((rows, rows), 1)
    later = (((ri // rows_per_page) == (rj // rows_per_page)) & (rj > ri)).astype(F32)
    o_ref[0] = _dot(x, within, HI) + _dot(later, _dot(x, total, HI), HI)


def _page_suffix_sums(logf_pool, layer):
    depth, n_pool = logf_pool.shape[0], logf_pool.shape[1]
    rows_per_page = PAGE_SIZE * N_HEADS // LANES
    pages = SUFFIX_PAGES if n_pool % SUFFIX_PAGES == 0 else n_pool
    rows = pages * rows_per_page
    x = logf_pool.reshape(depth, n_pool * rows_per_page, LANES)
    out = pl.pallas_call(
        _suffix_body,
        grid=(n_pool // pages,),
        in_specs=[pl.BlockSpec((1, rows, LANES), lambda i: (layer, i, 0))],
        out_specs=pl.BlockSpec((1, rows, LANES), lambda i: (0, i, 0)),
        out_shape=jax.ShapeDtypeStruct((1, n_pool * rows_per_page, LANES), F32),
        compiler_params=_cparams(1),
        name="page_suffix",
    )(x)
    return out.reshape(n_pool, PAGE_SIZE, N_HEADS).transpose(0, 2, 1)


def _decode_body(pt_ref, *refs, fox, n_tok, lam_init):
    pps = PAGES_PER_STEP
    if fox:
        q_ref, kn_ref, vn_ref, sm_ref, prow_ref = refs[:5]
        rest = refs[5:]
        k_refs, v_refs, r_refs = rest[:pps], rest[pps:2 * pps], rest[2 * pps:3 * pps]
        o_ref, logf_ref, qs_s, m_s, l_s, acc_s, rq_s, carry_s = rest[3 * pps:]
        groups = list(range(N_HEADS))
        group = HEAD_DIM
    else:
        q_ref, kn_ref, vn_ref, lam_ref, norm_ref = refs[:5]
        rest = refs[5:]
        k_refs, v_refs = rest[:pps], rest[pps:2 * pps]
        o_ref, qs_s, m_s, l_s, acc_s = rest[2 * pps:]
        groups = [2 * h for h in range(N_HEADS)] + [2 * h + 1 for h in range(N_HEADS)]
        group = DIFF_QK
    step = pl.program_id(1)
    rows = len(groups) * n_tok
    t = PAGE_SIZE

    @pl.when(step == 0)
    def _init():
        qs_s[...] = _stack_masked((q_ref[0] * (group ** -0.5)).astype(BF16), group, groups)
        m_s[...] = jnp.full((rows, 1), NEG, F32)
        l_s[...] = jnp.zeros((rows, 1), F32)
        acc_s[...] = jnp.zeros((rows, GROUP_WIDTH), F32)
        col = _iota((rows, t), 1)
        mask = (col < n_tok) & ((_iota((rows, t), 0) % n_tok) >= col)
        bias = None
        if fox:
            lf = -_softplus(-(sm_ref[0] + prow_ref[0:1, :]))
            logf_ref[0] = lf[:, 0:N_HEADS]
            after = _dot(_lower_tri(n_tok, strict=True).T.astype(F32), lf, HI)
            head_sel = (_iota((SUBLANES, LANES), 0) == _iota((SUBLANES, LANES), 1)).astype(F32)
            after_rows = _dot_nt(head_sel, _pad_rows(after, t), HI)
            rq = _stack_rows(after, n_tok)
            rq_s[...] = rq
            carry_s[...] = _stack_rows(jnp.broadcast_to(jnp.sum(lf, axis=0, keepdims=True), (n_tok, SMALL_W)),
                                       n_tok)
            bias = _stack_lanes(after_rows, n_tok) - rq
        _softmax_update(qs_s[...], _pad_rows(kn_ref[0].astype(BF16), t), _pad_rows(vn_ref[0].astype(BF16), t),
                        bias, mask, m_s, l_s, acc_s)

    for j in range(pps):
        bias = None
        if fox:
            incl = _stack_lanes(r_refs[j][0], n_tok)
            excl = jnp.where(_iota((rows, t), 1) == t - 1, 0.0, pltpu.roll(incl, t - 1, axis=1))
            carry = carry_s[...]
            bias = excl + (carry - rq_s[...])
            carry_s[...] = carry + incl[:, 0:1]
        _softmax_update(qs_s[...], k_refs[j][0, 0].astype(BF16), v_refs[j][0, 0].astype(BF16), bias, None,
                        m_s, l_s, acc_s)

    @pl.when(step == pl.num_programs(1) - 1)
    def _finish():
        o = acc_s[...] / l_s[...]
        if fox:
            o_ref[0] = _unstack_select(o, n_tok, HEAD_DIM, range(N_HEADS))
        else:
            o1 = _unstack_select(o, n_tok, HEAD_DIM, range(N_HEADS))
            o2 = _unstack_select(o, n_tok, HEAD_DIM, range(N_HEADS, 2 * N_HEADS))
            o_ref[0] = _diff_combine(o1, o2, lam_ref, norm_ref, lam_init)


def _decode_attention(page_table, q, k_new, v_new, k_pool, v_pool, layer, *, fox, extras, suffix=None,
                      lam_init=0.0):
    b, n_tok, w = q.shape
    n_pages = page_table.shape[1]
    pps = PAGES_PER_STEP
    assert n_pages % pps == 0 and n_tok % SUBLANES == 0
    steps = n_pages // pps
    tok_spec = pl.BlockSpec((1, n_tok, w), lambda i, s, pt: (i, 0, 0))
    const2 = lambda i, s, pt: (0, 0)

    def page_spec(j):
        return pl.BlockSpec((1, 1, PAGE_SIZE, w),
                            lambda i, s, pt: (layer, pt[i, n_pages - 1 - (s * pps + j)], 0, 0))

    def suffix_spec(j):
        return pl.BlockSpec((1, N_HEADS, PAGE_SIZE),
                            lambda i, s, pt: (pt[i, n_pages - 1 - (s * pps + j)], 0, 0))

    rows = (N_HEADS if fox else 2 * N_HEADS) * n_tok
    scratch = [pltpu.VMEM((rows, w), BF16), pltpu.VMEM((rows, 1), F32), pltpu.VMEM((rows, 1), F32),
               pltpu.VMEM((rows, w), F32)]
    in_specs = [tok_spec] * 3
    if fox:
        sm, prow = extras
        in_specs += [pl.BlockSpec((1, n_tok, SMALL_W), lambda i, s, pt: (i, 0, 0)),
                     pl.BlockSpec(prow.shape, const2)]
        in_specs += [page_spec(j) for j in range(pps)] * 2 + [suffix_spec(j) for j in range(pps)]
        operands = (q, k_new, v_new, sm, prow) + (k_pool,) * pps + (v_pool,) * pps + (suffix,) * pps
        out_shape = [jax.ShapeDtypeStruct((b, n_tok, w), F32), jax.ShapeDtypeStruct((b, n_tok, N_HEADS), F32)]
        out_specs = [tok_spec, pl.BlockSpec((1, n_tok, N_HEADS), lambda i, s, pt: (i, 0, 0))]
        scratch += [pltpu.VMEM((rows, 1), F32), pltpu.VMEM((rows, 1), F32)]
    else:
        lam, norm = extras
        in_specs += [pl.BlockSpec(lam.shape, const2), pl.BlockSpec(norm.shape, const2)]
        in_specs += [page_spec(j) for j in range(pps)] * 2
        operands = (q, k_new, v_new, lam, norm) + (k_pool,) * pps + (v_pool,) * pps
        out_shape = jax.ShapeDtypeStruct((b, n_tok, w), F32)
        out_specs = tok_spec
    return pl.pallas_call(
        functools.partial(_decode_body, fox=fox, n_tok=n_tok, lam_init=lam_init),
        grid_spec=pltpu.PrefetchScalarGridSpec(
            num_scalar_prefetch=1, grid=(b, steps), in_specs=in_specs, out_specs=out_specs,
            scratch_shapes=scratch),
        out_shape=out_shape,
        compiler_params=_cparams(2),
        name="decode_fox" if fox else "decode_diff",
    )(page_table, *operands)


def _split_w_in(w):
    a0, b0, c0, d0 = 0, A_COLS, A_COLS + B_COLS, A_COLS + B_COLS + C_COLS
    main = [w[:, a0:a0 + 768], w[:, b0:b0 + 768], w[:, c0:c0 + C_CONV_CH],
            w[:, c0 + C_CONV_CH:c0 + C_CONV_CH + 256], w[:, d0:d0 + 256], w[:, d0 + 256:d0 + 256 + D_CONV_CH]]
    small = jnp.concatenate([w[:, a0 + 768:a0 + 772], w[:, c0 + 1024:c0 + 1028], w[:, c0 + 1028:c0 + 1032],
                             w[:, d0 + 768:d0 + 772]], axis=1)
    small_pad = jnp.pad(small, ((0, 0), (0, SMALL_W - small.shape[1])))
    w_main = jnp.concatenate(main + [small_pad], axis=1).astype(BF16)
    return w_main, small.T.astype(BF16)


def _gdn_state_to_block(s):
    b = s.shape[0]
    eye = jnp.eye(N_HEADS, dtype=s.dtype)
    return jnp.einsum('bhkv,hg->bhkgv', s, eye).reshape(b, GROUP_WIDTH, GROUP_WIDTH)


def _gdn_block_to_state(sb):
    b = sb.shape[0]
    s5 = sb.reshape(b, N_HEADS, HEAD_DIM, N_HEADS, HEAD_DIM)
    return jnp.stack([s5[:, h, :, h, :] for h in range(N_HEADS)], axis=1)


def _ssd_state_to_block(s):
    b = s.shape[0]
    st = s.transpose(0, 3, 1, 2)
    grp = (jnp.arange(2)[:, None] == (jnp.arange(N_HEADS) // 2)[None, :]).astype(s.dtype)
    return jnp.einsum('bnhp,gh->bgnhp', st, grp).reshape(b, 2 * D_STATE, GROUP_WIDTH)


def _ssd_block_to_state(wb):
    b = wb.shape[0]
    w5 = wb.reshape(b, 2, D_STATE, N_HEADS, HEAD_DIM)
    s = jnp.stack([w5[:, h // 2, :, h, :] for h in range(N_HEADS)], axis=1)
    return s.transpose(0, 1, 3, 2)


def _row_tile(n):
    for tm in (768, 512, 384, 256, 128):
        if n % tm == 0:
            return tm
    return n


def _trunk(x, past, weights):
    (g_mix_pre, w_in, fox_forget_bias, diff_lambda, diff_norm, gdn_conv_w, gdn_A_log, gdn_dt_bias, gdn_norm,
     ssd_conv_w, ssd_conv_b, ssd_A_log, ssd_dt_bias, ssd_D, ssd_norm, w_out, g_mix_post, g_mlp_pre,
     w_mlp_up, w_mlp_down, g_mlp_post) = weights
    bsz, seq, d = x.shape
    n = bsz * seq
    tm = _row_tile(n)
    x2 = x.reshape(n, d)
    layer_rows = []
    for l in range(DEPTH):
        lam_init = 0.8 - 0.6 * math.exp(-0.3 * l)
        w_main, w_small_t = _split_w_in(w_in[l])
        outs = _in_proj(x2, g_mix_pre[l], w_main, w_small_t, tm)
        fq, fk, fv, dq, dk, dv, cqkv, cz, sz, sxbc, sm = [o.reshape(bsz, seq, -1) for o in outs[:-1]]

        bias_lanes = jnp.concatenate([fox_forget_bias[l], gdn_dt_bias[l], jnp.zeros((4,), F32), ssd_dt_bias[l]])
        prow = jnp.zeros((SUBLANES, SMALL_W), F32).at[0, 0:SMALL_ROWS].set(bias_lanes)
        pcol = jnp.zeros((SMALL_ROWS, SMALL_W), F32).at[:, 0].set(bias_lanes)
        lam = diff_lambda[l]
        dnorm = jnp.tile(diff_norm[l], N_HEADS).reshape(1, GROUP_WIDTH)
        gdn_par = jnp.zeros((SUBLANES, GROUP_WIDTH), F32)
        gdn_par = gdn_par.at[0].set(jnp.repeat(gdn_dt_bias[l], HEAD_DIM))
        gdn_par = gdn_par.at[1].set(jnp.repeat(gdn_A_log[l], HEAD_DIM))
        gdn_par = gdn_par.at[2].set(jnp.tile(gdn_norm[l], N_HEADS))
        ssd_par = jnp.zeros((SUBLANES, GROUP_WIDTH), F32)
        ssd_par = ssd_par.at[0].set(jnp.repeat(ssd_dt_bias[l], HEAD_DIM))
        ssd_par = ssd_par.at[1].set(jnp.repeat(ssd_A_log[l], HEAD_DIM))
        ssd_par = ssd_par.at[2].set(jnp.repeat(ssd_D[l], HEAD_DIM))
        ssd_par = ssd_par.at[3].set(ssd_norm[l])
        gdn_w8 = jnp.pad(gdn_conv_w[l], ((0, SUBLANES - CONV_W), (0, 0)))
        ssd_w8 = jnp.pad(ssd_conv_w[l], ((0, SUBLANES - CONV_W), (0, 0)))
        ssd_cb = ssd_conv_b[l].reshape(1, D_CONV_CH)
        state_pad = ((0, 0), (SUBLANES - (CONV_W - 1), 0), (0, 0))

        if past is None:
            seq_pad = -(-seq // ATTN_TILE) * ATTN_TILE
            smt = outs[-1].reshape(SMALL_ROWS, bsz, seq).transpose(1, 0, 2)
            smt = jnp.pad(smt, ((0, 0), (0, 0), (0, seq_pad - seq)))
            out_a, logf = _prompt_attention(fq, fk, fv, fox=True, extras=(sm, smt, prow, pcol))
            out_b = _prompt_attention(dq, dk, dv, fox=False, extras=(lam, dnorm), lam_init=lam_init)
            gdn_buf = jnp.zeros((bsz, SUBLANES, C_CONV_CH), F32)
            gdn_s0 = jnp.zeros((bsz, GROUP_WIDTH, GROUP_WIDTH), F32)
            ssd_buf = jnp.zeros((bsz, SUBLANES, D_CONV_CH), F32)
            ssd_s0 = jnp.zeros((bsz, 2 * D_STATE, GROUP_WIDTH), F32)
        else:
            (cache_fox_k, cache_fox_v, cache_fox_logf, cache_diff_k, cache_diff_v,
             state_gdn_conv, state_gdn, state_ssd_conv, state_ssd, page_table) = past
            pool = cache_fox_k.shape[1]
            as_pages = lambda c: c.reshape(c.shape[0], pool, PAGE_SIZE, GROUP_WIDTH)
            suffix = _page_suffix_sums(cache_fox_logf, l)
            out_a, logf = _decode_attention(page_table, fq, fk, fv, as_pages(cache_fox_k), as_pages(cache_fox_v),
                                            l, fox=True, extras=(sm, prow), suffix=suffix)
            out_b = _decode_attention(page_table, dq, dk, dv, as_pages(cache_diff_k), as_pages(cache_diff_v),
                                      l, fox=False, extras=(lam, dnorm), lam_init=lam_init)
            gdn_buf = jnp.pad(state_gdn_conv[l], state_pad)
            gdn_s0 = _gdn_state_to_block(state_gdn[l])
            ssd_buf = jnp.pad(state_ssd_conv[l], state_pad)
            ssd_s0 = _ssd_state_to_block(state_ssd[l])

        out_c, gdn_conv8, gdn_sb = _gdn(cqkv, cz, sm, gdn_buf, gdn_s0, gdn_w8, gdn_par)
        out_d, ssd_conv8, ssd_sb = _ssd(sxbc, sz, sm, ssd_buf, ssd_s0, ssd_w8, ssd_cb, ssd_par)

        mixer_outs = [o.reshape(n, GROUP_WIDTH) for o in (out_a, out_b, out_c, out_d)]
        x2 = _post(x2, mixer_outs, w_out[l].astype(BF16), g_mix_post[l], g_mlp_pre[l],
                   w_mlp_up[l].astype(BF16), w_mlp_down[l].astype(BF16), g_mlp_post[l], tm)

        tail = SUBLANES - (CONV_W - 1)
        layer_rows.append((
            fk.reshape(bsz, seq, N_HEADS, HEAD_DIM), fv.reshape(bsz, seq, N_HEADS, HEAD_DIM), logf,
            dk.reshape(bsz, seq, N_HEADS, 2, DIFF_QK), dv.reshape(bsz, seq, N_HEADS, HEAD_DIM),
            gdn_conv8[:, tail:], _gdn_block_to_state(gdn_sb),
            ssd_conv8[:, tail:], _ssd_block_to_state(ssd_sb)))
    stacked = tuple(jnp.stack([rows[i] for rows in layer_rows]) for i in range(len(layer_rows[0])))
    return x2.reshape(bsz, seq, d), stacked


def kernel(x_prompt, x_sample, cache_fox_k, cache_fox_v, cache_fox_logf, cache_diff_k, cache_diff_v, state_gdn_conv, state_gdn, state_ssd_conv, state_ssd, page_table, meta_tokens, g_mix_pre, w_in, fox_forget_bias, diff_lambda, diff_norm, gdn_conv_w, gdn_A_log, gdn_dt_bias, gdn_norm, ssd_conv_w, ssd_conv_b, ssd_A_log, ssd_dt_bias, ssd_D, ssd_norm, w_out, g_mix_post, g_mlp_pre, w_mlp_up, w_mlp_down, g_mlp_post):
    weights = (g_mix_pre, w_in, fox_forget_bias, diff_lambda, diff_norm, gdn_conv_w, gdn_A_log, gdn_dt_bias,
               gdn_norm, ssd_conv_w, ssd_conv_b, ssd_A_log, ssd_dt_bias, ssd_D, ssd_norm, w_out, g_mix_post,
               g_mlp_pre, w_mlp_up, w_mlp_down, g_mlp_post)
    bsz = x_prompt.shape[0]
    meta = jnp.broadcast_to(meta_tokens.astype(x_prompt.dtype)[None], (bsz,) + meta_tokens.shape)
    xp = jnp.concatenate([meta, x_prompt], axis=1)
    y_p, rows_p = _trunk(xp, None, weights)
    past = (cache_fox_k, cache_fox_v, cache_fox_logf, cache_diff_k, cache_diff_v,
            state_gdn_conv, state_gdn, state_ssd_conv, state_ssd, page_table)
    y_s, rows_s = _trunk(x_sample, past, weights)
    return (y_p[:, meta_tokens.shape[0]:], y_s) + rows_p + rows_s
```

```python
import functools
import math

import jax
import jax.numpy as jnp
from jax import lax
from jax.experimental import pallas as pl
from jax.experimental.pallas import tpu as pltpu

F32 = jnp.float32
BF16 = jnp.bfloat16

D_MODEL = 1024
DEPTH = 2
GROUP_WIDTH = 256
N_HEADS = 4
HEAD_DIM = 64
DIFF_QK = 32
D_STATE = 64
CONV_W = 4
C_CONV_CH = 3 * GROUP_WIDTH
D_CONV_CH = GROUP_WIDTH + 2 * 2 * D_STATE
A_COLS = 3 * GROUP_WIDTH + N_HEADS
B_COLS = 3 * GROUP_WIDTH
C_COLS = C_CONV_CH + GROUP_WIDTH + 2 * N_HEADS
D_FF = 4 * D_MODEL
PAGE_SIZE = 128
EPS = 1e-6
NEG = -1e30

LANES = 128
SUBLANES = 8
SMALL_W = 128
ATTN_Q_TILE = 128
ATTN_K_BLOCK = 256
GDN_CHUNK = 64
GDN_PRE_TILE = 256
SSD_CHUNK = 128
PAGES_PER_STEP = 16
SUFFIX_ROWS = 256
VMEM_LIMIT = 56 * 1024 * 1024

PROJ_WIDTHS = (256, 256, 256, 256, 256, 256, C_CONV_CH, 256, 256, D_CONV_CH, SMALL_W)


def _softplus(x):
    return jnp.maximum(x, 0.0) + jnp.log1p(jnp.exp(-jnp.abs(x)))


def _sigmoid(x):
    return 1.0 / (1.0 + jnp.exp(-x))


def _silu(x):
    return x * _sigmoid(x)


def _dot(a, b, prec=None):
    return jnp.dot(a, b, preferred_element_type=F32, precision=prec)


def _dot_nt(a, b, prec=None):
    return lax.dot_general(a, b, (((1,), (1,)), ((), ())), preferred_element_type=F32, precision=prec)


def _dot_tn(a, b, prec=None):
    return lax.dot_general(a, b, (((0,), (0,)), ((), ())), preferred_element_type=F32, precision=prec)


def _bf16_terms(x, terms):
    pieces = []
    r = x
    for t in range(terms):
        p = r.astype(BF16)
        pieces.append(p)
        if t + 1 < terms:
            r = r - p.astype(F32)
    return pieces


def _dot_split(x, w, terms, dot=_dot):
    w_bf = w.astype(BF16)
    acc = None
    for p in _bf16_terms(x, terms):
        part = dot(p, w_bf)
        acc = part if acc is None else acc + part
    return acc


def _dot_split_rhs(w, x, terms, dot=_dot):
    w_bf = w.astype(BF16)
    acc = None
    for p in _bf16_terms(x, terms):
        part = dot(w_bf, p)
        acc = part if acc is None else acc + part
    return acc


def _mm1(a, b):
    return _dot(a.astype(BF16), b.astype(BF16))


def _iota(shape, dim):
    return lax.broadcasted_iota(jnp.int32, shape, dim)


def _lane_group_mask(width, group, idx):
    return (_iota((1, width), 1) // group) == idx


def _block_ones(n, group):
    return ((_iota((n, n), 0) // group) == (_iota((n, n), 1) // group)).astype(F32)


def _lower_tri(n, strict=False):
    ri, ci = _iota((n, n), 0), _iota((n, n), 1)
    return (ri > ci) if strict else (ri >= ci)


def _stack_masked(x, group, groups):
    zero = jnp.zeros_like(x)
    return jnp.concatenate([jnp.where(_lane_group_mask(x.shape[1], group, g), x, zero) for g in groups], axis=0)


def _unstack_sum(xs, rows, n_groups):
    out = xs[0:rows]
    for g in range(1, n_groups):
        out = out + xs[g * rows:(g + 1) * rows]
    return out


def _unstack_select(xs, rows, group, blocks):
    out = None
    for g, blk in enumerate(blocks):
        part = jnp.where(_lane_group_mask(xs.shape[1], group, g), xs[blk * rows:(blk + 1) * rows], 0.0)
        out = part if out is None else out + part
    return out


def _pad_rows(x, rows):
    if x.shape[0] == rows:
        return x
    return jnp.concatenate([x, jnp.zeros((rows - x.shape[0], x.shape[1]), x.dtype)], axis=0)


def _cparams(n_grid_dims):
    return pltpu.CompilerParams(dimension_semantics=("arbitrary",) * n_grid_dims,
                                vmem_limit_bytes=VMEM_LIMIT)


def _in_proj_body(x_ref, g_ref, w_ref, *out_refs):
    x = x_ref[...]
    ms = jnp.mean(x * x, axis=-1, keepdims=True)
    h = (x * lax.rsqrt(ms + EPS) * g_ref[...]).astype(BF16)
    off = 0
    for o_ref, width in zip(out_refs, PROJ_WIDTHS):
        o_ref[...] = _dot(h, w_ref[:, off:off + width])
        off += width


def _in_proj(x2d, g, w_main, tm):
    n, d = x2d.shape
    assert n % tm == 0
    return pl.pallas_call(
        _in_proj_body,
        grid=(n // tm,),
        in_specs=[pl.BlockSpec((tm, d), lambda i: (i, 0)),
                  pl.BlockSpec((1, d), lambda i: (0, 0)),
                  pl.BlockSpec(w_main.shape, lambda i: (0, 0))],
        out_specs=[pl.BlockSpec((tm, w), lambda i: (i, 0)) for w in PROJ_WIDTHS],
        out_shape=[jax.ShapeDtypeStruct((n, w), F32) for w in PROJ_WIDTHS],
        compiler_params=_cparams(1),
        name="in_proj",
    )(x2d, g.reshape(1, d), w_main)


def _post_body(x_ref, oa_ref, ob_ref, oc_ref, od_ref, wo_ref, gpost_ref, gpre_ref, wu_ref, wd_ref,
               gmlp_ref, y_ref, *, ff_chunk):
    mix = None
    for i, o_ref in enumerate((oa_ref, ob_ref, oc_ref, od_ref)):
        part = _dot(o_ref[...].astype(BF16), wo_ref[i * GROUP_WIDTH:(i + 1) * GROUP_WIDTH, :])
        mix = part if mix is None else mix + part
    ms = jnp.mean(mix * mix, axis=-1, keepdims=True)
    x1 = x_ref[...] + mix * lax.rsqrt(ms + EPS) * gpost_ref[...]
    ms = jnp.mean(x1 * x1, axis=-1, keepdims=True)
    h = (x1 * lax.rsqrt(ms + EPS) * gpre_ref[...]).astype(BF16)
    acc = None
    for c in range(D_FF // ff_chunk):
        u = jnp.maximum(_dot(h, wu_ref[:, c * ff_chunk:(c + 1) * ff_chunk]), 0.0)
        part = _dot((u * u).astype(BF16), wd_ref[c * ff_chunk:(c + 1) * ff_chunk, :])
        acc = part if acc is None else acc + part
    ms = jnp.mean(acc * acc, axis=-1, keepdims=True)
    y_ref[...] = x1 + acc * lax.rsqrt(ms + EPS) * gmlp_ref[...]


def _post(x2d, outs, w_out, g_post, g_pre, w_up, w_down, g_mlp, tm):
    n, d = x2d.shape
    assert n % tm == 0
    row = lambda i: (i, 0)
    const = lambda i: (0, 0)
    single = pl.Buffered(1)
    return pl.pallas_call(
        functools.partial(_post_body, ff_chunk=512),
        grid=(n // tm,),
        in_specs=[pl.BlockSpec((tm, d), row)]
        + [pl.BlockSpec((tm, GROUP_WIDTH), row)] * 4
        + [pl.BlockSpec(w_out.shape, const, pipeline_mode=single),
           pl.BlockSpec((1, d), const), pl.BlockSpec((1, d), const),
           pl.BlockSpec(w_up.shape, const, pipeline_mode=single),
           pl.BlockSpec(w_down.shape, const, pipeline_mode=single),
           pl.BlockSpec((1, d), const)],
        out_specs=pl.BlockSpec((tm, d), row),
        out_shape=jax.ShapeDtypeStruct((n, d), F32),
        compiler_params=_cparams(1),
        name="post_mlp",
    )(x2d, *outs, w_out, g_post.reshape(1, d), g_pre.reshape(1, d), w_up, w_down, g_mlp.reshape(1, d))


def _diff_lambda(lam_ref, lam_init):
    lp = lam_ref[...]
    return (jnp.exp(jnp.sum(lp[0:1] * lp[1:2], axis=-1, keepdims=True))
            - jnp.exp(jnp.sum(lp[2:3] * lp[3:4], axis=-1, keepdims=True)) + lam_init)


def _stack_rows(x, size):
    return jnp.concatenate([x[:, h:h + 1] for h in range(N_HEADS)], axis=0)


def _stack_lanes(x, size):
    return jnp.concatenate([jnp.broadcast_to(x[h:h + 1, :], (size, x.shape[1])) for h in range(N_HEADS)],
                           axis=0)


def _prompt_attn_body(*refs, fox, seq, lam_init):
    if fox:
        (q_ref, k_ref, v_ref, sm_ref, prow_ref, o_ref, logf_ref,
         kb_s, qt_s, vt_s, m_s, l_s, acc_s, ckx_s) = refs
        groups = list(range(N_HEADS))
        group = HEAD_DIM
    else:
        (q_ref, k_ref, v_ref, lam_ref, norm_ref, o_ref, kb_s, qt_s, vt_s, m_s, l_s, acc_s) = refs
        groups = [2 * h for h in range(N_HEADS)] + [2 * h + 1 for h in range(N_HEADS)]
        group = DIFF_QK
    tq, tk = ATTN_Q_TILE, ATTN_K_BLOCK
    n_q = -(-seq // tq)
    seq_pad = kb_s.shape[0]
    last_valid = seq - (n_q - 1) * tq
    n_blocks = len(groups)
    width = n_blocks * tq

    kb_s[0:seq, :] = k_ref[0].astype(BF16)
    kb_s[seq:seq_pad, :] = jnp.zeros((seq_pad - seq, GROUP_WIDTH), BF16)
    if seq_pad > n_q * tq:
        zeros = jnp.zeros((GROUP_WIDTH, seq_pad - n_q * tq), BF16)
        qt_s[:, n_q * tq:seq_pad] = zeros
        vt_s[:, n_q * tq:seq_pad] = zeros
    if fox:
        if seq_pad > n_q * tq:
            ckx_s[n_q * tq:seq_pad, :] = jnp.zeros((seq_pad - n_q * tq, width), F32)
        bias_row = prow_ref[0:1, :]
        tri_lo = _lower_tri(tq).astype(F32)

    def stage_tile(start, valid, carry):
        qt_s[:, pl.ds(start, tq)] = (_pad_rows(q_ref[0, pl.ds(start, valid), :], tq)
                                     * (group ** -0.5)).T.astype(BF16)
        vt_s[:, pl.ds(start, tq)] = _pad_rows(v_ref[0, pl.ds(start, valid), :], tq).T.astype(BF16)
        if not fox:
            return carry
        lf = -_softplus(-(_pad_rows(sm_ref[0, pl.ds(start, valid), :], tq) + bias_row))
        logf_ref[0, pl.ds(start, valid), :] = lf[0:valid, 0:N_HEADS]
        cc = _dot_split_rhs(tri_lo, lf, 3) + carry
        ckx_s[pl.ds(start, tq), :] = jnp.concatenate(
            [jnp.broadcast_to(cc[:, h:h + 1], (tq, tq)) for h in range(N_HEADS)], axis=1)
        return cc[tq - 1:tq, :]

    carry = jnp.zeros((1, SMALL_W), F32)
    carry = lax.fori_loop(0, n_q - 1, lambda i, c: stage_tile(pl.multiple_of(i * tq, tq), tq, c), carry)
    stage_tile((n_q - 1) * tq, last_valid, carry)

    row_group = _iota((GROUP_WIDTH, 1), 0) // group
    lane_q = _iota((1, width), 1) % tq
    key_off = _iota((tk, 1), 0)

    def q_tile(idx, valid):
        static = isinstance(idx, int)
        qs0 = idx * tq if static else pl.multiple_of(idx * tq, tq)
        qt = qt_s[:, pl.ds(qs0, tq)]
        zero = jnp.zeros_like(qt)
        qstack = jnp.concatenate([jnp.where(row_group == g, qt, zero) for g in groups], axis=1)
        m_s[...] = jnp.full(m_s.shape, NEG, F32)
        l_s[...] = jnp.zeros(l_s.shape, F32)
        acc_s[...] = jnp.zeros(acc_s.shape, F32)

        def block(kb0, masked):
            s = _dot(kb_s[pl.ds(kb0, tk), :], qstack)
            if fox:
                s = s - ckx_s[pl.ds(kb0, tk), :]
            if masked:
                s = jnp.where((kb0 + key_off) <= (qs0 + lane_q), s, NEG)
            m_old = m_s[...]
            m_new = jnp.maximum(m_old, jnp.max(s, axis=0, keepdims=True))
            alpha = jnp.exp(m_old - m_new)
            p = jnp.exp(s - m_new)
            l_s[...] = alpha * l_s[...] + jnp.sum(p, axis=0, keepdims=True)
            m_s[...] = m_new
            pb = p.astype(BF16)
            for b in range(n_blocks):
                h = b % N_HEADS
                upd = _dot(vt_s[h * HEAD_DIM:(h + 1) * HEAD_DIM, pl.ds(kb0, tk)], pb[:, b * tq:(b + 1) * tq])
                rows = slice(b * HEAD_DIM, (b + 1) * HEAD_DIM)
                acc_s[rows, :] = acc_s[rows, :] * alpha[:, b * tq:(b + 1) * tq] + upd

        n_before = (idx * tq) // tk if static else lax.div(idx * tq, tk)

        def full_step(j, c):
            block(pl.multiple_of(j * tk, tk), False)
            return c

        lax.fori_loop(0, n_before, full_step, 0)
        block(n_before * tk if static else pl.multiple_of(n_before * tk, tk), True)

        l = l_s[...]
        o_t = jnp.concatenate([acc_s[b * HEAD_DIM:(b + 1) * HEAD_DIM, :] / l[:, b * tq:(b + 1) * tq]
                               for b in range(n_blocks)], axis=0)
        if fox:
            o = o_t.T
        else:
            od = o_t[0:GROUP_WIDTH] - _diff_lambda(lam_ref, lam_init) * o_t[GROUP_WIDTH:]
            normed = []
            for h in range(N_HEADS):
                blk = od[h * HEAD_DIM:(h + 1) * HEAD_DIM]
                ms = jnp.mean(blk * blk, axis=0, keepdims=True)
                normed.append(blk * lax.rsqrt(ms + EPS))
            o = jnp.concatenate(normed, axis=0).T * norm_ref[...] * (1.0 - lam_init)
        o_ref[0, pl.ds(qs0, valid), :] = o[0:valid]

    def loop_body(i, c):
        q_tile(i, tq)
        return c

    lax.fori_loop(0, n_q - 1, loop_body, 0)
    q_tile(n_q - 1, last_valid)


def _prompt_attention(q, k, v, *, fox, extras, lam_init=0.0):
    b, seq, w = q.shape
    tq, tk = ATTN_Q_TILE, ATTN_K_BLOCK
    n_q = -(-seq // tq)
    seq_pad = -(-(n_q * tq) // tk) * tk
    n_blocks = N_HEADS if fox else 2 * N_HEADS
    seq_spec = pl.BlockSpec((1, seq, w), lambda i: (i, 0, 0))
    const2 = lambda i: (0, 0)
    scratch = [pltpu.VMEM((seq_pad, w), BF16), pltpu.VMEM((w, seq_pad), BF16), pltpu.VMEM((w, seq_pad), BF16),
               pltpu.VMEM((1, n_blocks * tq), F32), pltpu.VMEM((1, n_blocks * tq), F32),
               pltpu.VMEM((n_blocks * HEAD_DIM, tq), F32)]
    if fox:
        sm, prow = extras
        in_specs = [seq_spec] * 3 + [pl.BlockSpec((1, seq, SMALL_W), lambda i: (i, 0, 0)),
                                     pl.BlockSpec(prow.shape, const2)]
        out_shape = [jax.ShapeDtypeStruct((b, seq, w), F32), jax.ShapeDtypeStruct((b, seq, N_HEADS), F32)]
        out_specs = [seq_spec, pl.BlockSpec((1, seq, N_HEADS), lambda i: (i, 0, 0))]
        scratch = scratch + [pltpu.VMEM((seq_pad, n_blocks * tq), F32)]
    else:
        lam, norm = extras
        in_specs = [seq_spec] * 3 + [pl.BlockSpec(lam.shape, const2), pl.BlockSpec(norm.shape, const2)]
        out_shape = jax.ShapeDtypeStruct((b, seq, w), F32)
        out_specs = seq_spec
    return pl.pallas_call(
        functools.partial(_prompt_attn_body, fox=fox, seq=seq, lam_init=lam_init),
        grid=(b,),
        in_specs=in_specs,
        out_specs=out_specs,
        out_shape=out_shape,
        scratch_shapes=scratch,
        compiler_params=_cparams(1),
        name="prompt_fox" if fox else "prompt_diff",
    )(q, k, v, *extras)


def _fill_conv_scratch(x_ref, buf_ref, xpad_s, new_ref, seq):
    xpad_s[0:SUBLANES, :] = buf_ref[0]
    xpad_s[SUBLANES:SUBLANES + seq, :] = x_ref[0]
    tail = xpad_s.shape[0] - SUBLANES - seq
    if tail:
        xpad_s[SUBLANES + seq:, :] = jnp.zeros((tail, xpad_s.shape[1]), F32)
    new_ref[0] = xpad_s[seq:seq + SUBLANES, :]


def _conv_chunk(xpad_s, w_ref, start, size):
    n = size + SUBLANES
    win = xpad_s[pl.ds(start, n), :]
    out = None
    for i in range(CONV_W):
        first = SUBLANES - (CONV_W - 1) + i
        part = pltpu.roll(win, n - first, axis=0)[0:size] * w_ref[i:i + 1, :]
        out = part if out is None else out + part
    return out


def _head_selector(first_lanes):
    parts = [(_iota((SMALL_W, GROUP_WIDTH), 0) == first + _iota((SMALL_W, GROUP_WIDTH), 1) // HEAD_DIM)
             for first in first_lanes]
    return jnp.concatenate(parts, axis=1).astype(F32)


def _expand_heads(sm, first_lane):
    return _dot_split(sm, _head_selector([first_lane]), 3)


def _head_rows(x_exp):
    sel = (_iota((SUBLANES, GROUP_WIDTH), 0) == _iota((SUBLANES, GROUP_WIDTH), 1) // HEAD_DIM)
    return _dot_split_rhs(sel.astype(F32) * (1.0 / HEAD_DIM), x_exp, 3, _dot_nt)


def _inv_unit_lower(a, n, blk):
    ri = _iota((n, n), 0)
    ci = _iota((n, n), 1)
    eye = (ri == ci).astype(F32)
    base = min(16, blk)
    bneg = -jnp.where((ri // base) == (ci // base), a, 0.0)
    x = eye + bneg
    p = bneg
    s = 1
    while 2 * s < base:
        p = _mm1(p, p)
        x = x + _mm1(x, p)
        s *= 2
    size = base
    while size < blk:
        c = jnp.where(((ri // (2 * size)) == (ci // (2 * size))) & ((ri // size) != (ci // size)), a, 0.0)
        x = x - _mm1(_mm1(x, c), x)
        size *= 2
    return x


def _tile_schedule(seq, tile, fn):
    n_tiles = -(-seq // tile)
    if n_tiles > 1:
        def body(i, c):
            fn(pl.multiple_of(i * tile, tile), tile)
            return c
        lax.fori_loop(0, n_tiles - 1, body, 0)
    fn((n_tiles - 1) * tile, seq - (n_tiles - 1) * tile)


def _gdn_body(x_ref, z_ref, sm_ref, buf_ref, s0_ref, w_ref, par_ref,
              o_ref, new_ref, sn_ref, xpad_s, q_s, k_s, v_s, g_s, b_s, st_s, *, seq, pre_tile):
    _fill_conv_scratch(x_ref, buf_ref, xpad_s, new_ref, seq)
    st_s[...] = s0_ref[0]
    size = GDN_CHUNK
    n = N_HEADS * size
    dtb = par_ref[0:1, :]
    a_neg = -jnp.exp(par_ref[1:2, :])
    gain = par_ref[2:3, :]
    head_ones = _block_ones(GROUP_WIDTH, HEAD_DIM)
    heads = range(N_HEADS)

    def pre(start, valid):
        act = _silu(_conv_chunk(xpad_s, w_ref, start, pre_tile))
        q = act[:, 0:GROUP_WIDTH]
        k = act[:, GROUP_WIDTH:2 * GROUP_WIDTH]
        q_s[pl.ds(start, pre_tile), :] = (q * lax.rsqrt(_dot_split(q * q, head_ones, 2) + EPS)
                                          * (HEAD_DIM ** -0.5))
        k_s[pl.ds(start, pre_tile), :] = k * lax.rsqrt(_dot_split(k * k, head_ones, 2) + EPS)
        v_s[pl.ds(start, pre_tile), :] = act[:, 2 * GROUP_WIDTH:3 * GROUP_WIDTH]
        ab = _dot_split(_pad_rows(sm_ref[0, pl.ds(start, valid), :], pre_tile), _head_selector([4, 8]), 3)
        g = a_neg * _softplus(ab[:, 0:GROUP_WIDTH] + dtb)
        beta = _sigmoid(ab[:, GROUP_WIDTH:])
        if valid < pre_tile:
            live = _iota((pre_tile, 1), 0) < valid
            g = jnp.where(live, g, 0.0)
            beta = jnp.where(live, beta, 0.0)
        g_s[pl.ds(start, pre_tile), :] = g
        b_s[pl.ds(start, pre_tile), :] = beta

    _tile_schedule(seq, pre_tile, pre)

    ri = _iota((n, n), 0)
    ci = _iota((n, n), 1)
    same = (ri // size) == (ci // size)
    tri = _lower_tri(size).astype(F32)
    mean_row = jnp.full((SUBLANES, GROUP_WIDTH), 1.0 / HEAD_DIM, F32)

    def chunk(start, valid):
        rows = pl.ds(start, size)
        q, k, v, beta = q_s[rows, :], k_s[rows, :], v_s[rows, :], b_s[rows, :]
        gam = _dot_split_rhs(tri, g_s[rows, :], 3)
        egam = jnp.exp(gam)
        gcol = jnp.concatenate([gam[:, h * HEAD_DIM:h * HEAD_DIM + 1] for h in heads], axis=0)
        grow = _dot_split_rhs(mean_row, _stack_masked(gam, HEAD_DIM, heads), 3, _dot_nt)[0:1]
        bcol = jnp.concatenate([beta[:, h * HEAD_DIM:h * HEAD_DIM + 1] for h in heads], axis=0)
        decay = jnp.exp(jnp.where(same & (ri >= ci), gcol - grow, -jnp.inf))
        ks = _stack_masked(k, HEAD_DIM, heads).astype(BF16)
        qs = _stack_masked(q, HEAD_DIM, heads).astype(BF16)
        a_mat = jnp.where(same & (ri > ci), _dot_nt(ks, ks) * decay, 0.0) * bcol
        t_inv = _inv_unit_lower(a_mat, n, size)
        rhs = jnp.concatenate([_stack_masked(v * beta, HEAD_DIM, heads),
                               _stack_masked(k * egam * beta, HEAD_DIM, heads)], axis=1)
        uw = _mm1(t_inv, rhs)
        u = _unstack_sum(uw[:, 0:GROUP_WIDTH], size, N_HEADS)
        w = _unstack_sum(uw[:, GROUP_WIDTH:], size, N_HEADS)
        state = st_s[...]
        state_bf = state.astype(BF16)
        v_new = u - _dot(w.astype(BF16), state_bf)
        pm = (_dot_nt(qs, ks) * decay).astype(BF16)
        o = (_unstack_sum(_dot(pm, _stack_masked(v_new, HEAD_DIM, heads).astype(BF16)), size, N_HEADS)
             + _dot((q * egam).astype(BF16), state_bf))
        g_last = gam[size - 1:size, :]
        k_dec = (k * jnp.exp(g_last - gam)).astype(BF16)
        st_s[...] = state * jnp.exp(g_last) + _dot_tn(k_dec, v_new.astype(BF16)) * head_ones
        ms = _dot_split(o * o, head_ones, 2) * (1.0 / HEAD_DIM)
        y = o * lax.rsqrt(ms + EPS) * gain
        o_ref[0, pl.ds(start, valid), :] = y[0:valid] * _silu(z_ref[0, pl.ds(start, valid), :])

    _tile_schedule(seq, size, chunk)
    sn_ref[0] = st_s[...]


def _gdn(x, z, sm, buf8, s0, conv_w8, par):
    b, seq, ch = x.shape
    pre_tile = GDN_PRE_TILE if seq > GDN_CHUNK else GDN_CHUNK
    rows_pad = -(-seq // pre_tile) * pre_tile
    bspec = lambda shape: pl.BlockSpec((1,) + shape, lambda i: (i, 0, 0))
    const2 = lambda i: (0, 0)
    return pl.pallas_call(
        functools.partial(_gdn_body, seq=seq, pre_tile=pre_tile),
        grid=(b,),
        in_specs=[bspec((seq, ch)), bspec((seq, GROUP_WIDTH)), bspec((seq, SMALL_W)), bspec((SUBLANES, ch)),
                  bspec((GROUP_WIDTH, GROUP_WIDTH)), pl.BlockSpec(conv_w8.shape, const2),
                  pl.BlockSpec(par.shape, const2)],
        out_specs=[bspec((seq, GROUP_WIDTH)), bspec((SUBLANES, ch)), bspec((GROUP_WIDTH, GROUP_WIDTH))],
        out_shape=[jax.ShapeDtypeStruct((b, seq, GROUP_WIDTH), F32),
                   jax.ShapeDtypeStruct((b, SUBLANES, ch), F32),
                   jax.ShapeDtypeStruct((b, GROUP_WIDTH, GROUP_WIDTH), F32)],
        scratch_shapes=[pltpu.VMEM((rows_pad + 2 * SUBLANES, ch), F32)]
        + [pltpu.VMEM((rows_pad, GROUP_WIDTH), F32)] * 5
        + [pltpu.VMEM((GROUP_WIDTH, GROUP_WIDTH), F32)],
        compiler_params=_cparams(1),
        name="gdn",
    )(x, z, sm, buf8, s0, conv_w8, par)


def _ssd_body(x_ref, z_ref, sm_ref, buf_ref, s0_ref, w_ref, cb_ref, par_ref,
              o_ref, new_ref, sn_ref, xpad_s, st_s, *, seq):
    _fill_conv_scratch(x_ref, buf_ref, xpad_s, new_ref, seq)
    st_s[...] = s0_ref[0]
    size = SSD_CHUNK
    dtb = par_ref[0:1, :]
    a_neg = -jnp.exp(par_ref[1:2, :])
    d_skip = par_ref[2:3, :]
    gain = par_ref[3:4, :]
    n_state = 2 * D_STATE
    live_state = ((_iota((n_state, GROUP_WIDTH), 0) // D_STATE)
                  == (_iota((n_state, GROUP_WIDTH), 1) // (2 * HEAD_DIM))).astype(F32)

    def chunk(start, valid):
        pre = _silu(_conv_chunk(xpad_s, w_ref, start, size) + cb_ref[...])
        sx = pre[:, 0:GROUP_WIDTH]
        sb = pre[:, GROUP_WIDTH:GROUP_WIDTH + n_state]
        sc = pre[:, GROUP_WIDTH + n_state:]
        sm = _pad_rows(sm_ref[0, pl.ds(start, valid), :], size)
        dt = _softplus(_expand_heads(sm, 12) + dtb)
        if valid < size:
            dt = jnp.where(_iota((size, 1), 0) < valid, dt, 0.0)
        a = dt * a_neg
        tri = _lower_tri(size)
        cum = _dot_split_rhs(tri.astype(F32), a, 3)
        ecum = jnp.exp(cum)
        cum_rows = _head_rows(cum)
        xdt = sx * dt
        sb_bf = sb.astype(BF16)
        cb = [_dot_nt(jnp.where(_lane_group_mask(n_state, D_STATE, g), sc, 0.0).astype(BF16), sb_bf)
              for g in range(2)]
        y = None
        for h in range(N_HEADS):
            dec = jnp.exp(jnp.where(tri, cum[:, h * HEAD_DIM:h * HEAD_DIM + 1] - cum_rows[h:h + 1, :], -jnp.inf))
            xh = jnp.where(_lane_group_mask(GROUP_WIDTH, HEAD_DIM, h), xdt, 0.0).astype(BF16)
            part = _dot((cb[h // 2] * dec).astype(BF16), xh)
            y = part if y is None else y + part
        state = st_s[...]
        y = y + _dot(sc.astype(BF16), state.astype(BF16)) * ecum
        c_last = cum[size - 1:size, :]
        st_s[...] = (state * jnp.exp(c_last)
                     + _dot_tn(sb_bf, (xdt * jnp.exp(c_last - cum)).astype(BF16)) * live_state)
        ys = (y + d_skip * sx)[0:valid] * _silu(z_ref[0, pl.ds(start, valid), :])
        ms = jnp.mean(ys * ys, axis=-1, keepdims=True)
        o_ref[0, pl.ds(start, valid), :] = ys * lax.rsqrt(ms + EPS) * gain

    _tile_schedule(seq, size, chunk)
    sn_ref[0] = st_s[...]


def _ssd(x, z, sm, buf8, s0, conv_w8, conv_b, par):
    b, seq, ch = x.shape
    n_state = 2 * D_STATE
    seq_pad = -(-seq // SSD_CHUNK) * SSD_CHUNK
    bspec = lambda shape: pl.BlockSpec((1,) + shape, lambda i: (i, 0, 0))
    const2 = lambda i: (0, 0)
    return pl.pallas_call(
        functools.partial(_ssd_body, seq=seq),
        grid=(b,),
        in_specs=[bspec((seq, ch)), bspec((seq, GROUP_WIDTH)), bspec((seq, SMALL_W)), bspec((SUBLANES, ch)),
                  bspec((n_state, GROUP_WIDTH)), pl.BlockSpec(conv_w8.shape, const2),
                  pl.BlockSpec(conv_b.shape, const2), pl.BlockSpec(par.shape, const2)],
        out_specs=[bspec((seq, GROUP_WIDTH)), bspec((SUBLANES, ch)), bspec((n_state, GROUP_WIDTH))],
        out_shape=[jax.ShapeDtypeStruct((b, seq, GROUP_WIDTH), F32),
                   jax.ShapeDtypeStruct((b, SUBLANES, ch), F32),
                   jax.ShapeDtypeStruct((b, n_state, GROUP_WIDTH), F32)],
        scratch_shapes=[pltpu.VMEM((seq_pad + 2 * SUBLANES, ch), F32), pltpu.VMEM((n_state, GROUP_WIDTH), F32)],
        compiler_params=_cparams(1),
        name="ssd",
    )(x, z, sm, buf8, s0, conv_w8, conv_b, par)


def _suffix_body(x_ref, o_ref):
    later = (_iota((PAGE_SIZE, PAGE_SIZE), 0) >= _iota((PAGE_SIZE, PAGE_SIZE), 1)).astype(F32)
    o_ref[0] = _dot_split(x_ref[0], later, 3)


def _page_suffix_sums(logf_rows, layer):
    n_rows = logf_rows.shape[1]
    rows = SUFFIX_ROWS if n_rows % SUFFIX_ROWS == 0 else n_rows
    out = pl.pallas_call(
        _suffix_body,
        grid=(n_rows // rows,),
        in_specs=[pl.BlockSpec((1, rows, PAGE_SIZE), lambda i: (layer, i, 0))],
        out_specs=pl.BlockSpec((1, rows, PAGE_SIZE), lambda i: (0, i, 0)),
        out_shape=jax.ShapeDtypeStruct((1, n_rows, PAGE_SIZE), F32),
        compiler_params=_cparams(1),
        name="page_suffix",
    )(logf_rows)
    return out.reshape(n_rows // N_HEADS, N_HEADS, PAGE_SIZE)


def _decode_body(pt_ref, *refs, fox, n_tok, lam_init):
    pps = PAGES_PER_STEP
    if fox:
        q_ref, kn_ref, vn_ref, sm_ref, prow_ref = refs[:5]
        rest = refs[5:]
        k_refs, v_refs, r_refs = rest[:pps], rest[pps:2 * pps], rest[2 * pps:3 * pps]
        o_ref, logf_ref, qs_s, m_s, l_s, acc_s, rq_s, carry_s = rest[3 * pps:]
        groups = list(range(N_HEADS))
        group = HEAD_DIM
    else:
        q_ref, kn_ref, vn_ref, lam_ref, norm_ref = refs[:5]
        rest = refs[5:]
        k_refs, v_refs = rest[:pps], rest[pps:2 * pps]
        o_ref, qs_s, m_s, l_s, acc_s = rest[2 * pps:]
        groups = [2 * h for h in range(N_HEADS)] + [2 * h + 1 for h in range(N_HEADS)]
        group = DIFF_QK
    step = pl.program_id(1)
    rows = len(groups) * n_tok
    t = PAGE_SIZE

    def update(scores, value_products):
        m_old = m_s[...]
        top = scores[0]
        for s in scores[1:]:
            top = jnp.maximum(top, s)
        m_new = jnp.maximum(m_old, jnp.max(top, axis=-1, keepdims=True))
        alpha = jnp.exp(m_old - m_new)
        probs = [jnp.exp(s - m_new) for s in scores]
        total = probs[0]
        for p in probs[1:]:
            total = total + p
        l_s[...] = alpha * l_s[...] + jnp.sum(total, axis=-1, keepdims=True)
        pv = None
        for p, product in zip(probs, value_products):
            part = product(p.astype(BF16))
            pv = part if pv is None else pv + part
        acc_s[...] = alpha * acc_s[...] + pv
        m_s[...] = m_new

    @pl.when(step == 0)
    def _init():
        qs_s[...] = _stack_masked((q_ref[0] * (group ** -0.5)).astype(BF16), group, groups)
        m_s[...] = jnp.full((rows, 1), NEG, F32)
        l_s[...] = jnp.zeros((rows, 1), F32)
        acc_s[...] = jnp.zeros((rows, GROUP_WIDTH), F32)
        col = _iota((rows, t), 1)
        mask = (col < n_tok) & ((_iota((rows, t), 0) % n_tok) >= col)
        s = _dot_nt(qs_s[...], _pad_rows(kn_ref[0].astype(BF16), t))
        if fox:
            lf = -_softplus(-(sm_ref[0] + prow_ref[0:1, :]))
            logf_ref[0] = lf[:, 0:N_HEADS]
            later = (_iota((n_tok, n_tok), 0) < _iota((n_tok, n_tok), 1)).astype(F32)
            after = _dot_split_rhs(later, lf, 3)
            head_sel = (_iota((SUBLANES, LANES), 0) == _iota((SUBLANES, LANES), 1)).astype(F32)
            after_rows = _dot_split_rhs(head_sel, _pad_rows(after, t), 3, _dot_nt)
            rq = _stack_rows(after, n_tok)
            rq_s[...] = rq
            carry_s[...] = _stack_rows(jnp.broadcast_to(jnp.sum(lf, axis=0, keepdims=True), (n_tok, SMALL_W)),
                                       n_tok)
            s = s + (_stack_lanes(after_rows, n_tok) - rq)
        v_new = _pad_rows(vn_ref[0].astype(BF16), t)
        update([jnp.where(mask, s, NEG)], [lambda p: _dot(p, v_new)])

    qs = qs_s[...]
    scores = []
    products = []
    if fox:
        carry = carry_s[...]
        rq = rq_s[...]
    for j in range(pps):
        s = _dot(qs, k_refs[j][0, 0].astype(BF16))
        if fox:
            incl = _stack_lanes(r_refs[j][0], n_tok)
            excl = jnp.where(_iota((rows, t), 1) == t - 1, 0.0, pltpu.roll(incl, t - 1, axis=1))
            s = s + (excl + (carry - rq))
            carry = carry + incl[:, 0:1]
        scores.append(s)
        products.append(functools.partial(lambda p, ref: _dot_nt(p, ref[0, 0].astype(BF16)), ref=v_refs[j]))
    if fox:
        carry_s[...] = carry
    update(scores, products)

    @pl.when(step == pl.num_programs(1) - 1)
    def _finish():
        o = acc_s[...] / l_s[...]
        if fox:
            o_ref[0] = _unstack_select(o, n_tok, HEAD_DIM, range(N_HEADS))
        else:
            o1 = _unstack_select(o, n_tok, HEAD_DIM, range(N_HEADS))
            o2 = _unstack_select(o, n_tok, HEAD_DIM, range(N_HEADS, 2 * N_HEADS))
            od = o1 - _diff_lambda(lam_ref, lam_init) * o2
            ms = _dot_split(od * od, _block_ones(GROUP_WIDTH, HEAD_DIM), 2) * (1.0 / HEAD_DIM)
            o_ref[0] = od * lax.rsqrt(ms + EPS) * norm_ref[...] * (1.0 - lam_init)


def _decode_attention(page_table, q, k_new, v_new, k_pool, v_pool, layer, *, fox, extras, suffix=None,
                      lam_init=0.0):
    b, n_tok, w = q.shape
    n_pages = page_table.shape[1]
    pps = PAGES_PER_STEP
    assert n_pages % pps == 0 and n_tok % SUBLANES == 0
    steps = n_pages // pps
    tok_spec = pl.BlockSpec((1, n_tok, w), lambda i, s, pt: (i, 0, 0))
    const2 = lambda i, s, pt: (0, 0)

    def page_spec(j):
        return pl.BlockSpec((1, 1, w, PAGE_SIZE),
                            lambda i, s, pt: (layer, pt[i, n_pages - 1 - (s * pps + j)], 0, 0))

    def suffix_spec(j):
        return pl.BlockSpec((1, N_HEADS, PAGE_SIZE),
                            lambda i, s, pt: (pt[i, n_pages - 1 - (s * pps + j)], 0, 0))

    rows = (N_HEADS if fox else 2 * N_HEADS) * n_tok
    scratch = [pltpu.VMEM((rows, w), BF16), pltpu.VMEM((rows, 1), F32), pltpu.VMEM((rows, 1), F32),
               pltpu.VMEM((rows, w), F32)]
    in_specs = [tok_spec] * 3
    if fox:
        sm, prow = extras
        in_specs += [pl.BlockSpec((1, n_tok, SMALL_W), lambda i, s, pt: (i, 0, 0)),
                     pl.BlockSpec(prow.shape, const2)]
        in_specs += [page_spec(j) for j in range(pps)] * 2 + [suffix_spec(j) for j in range(pps)]
        operands = (q, k_new, v_new, sm, prow) + (k_pool,) * pps + (v_pool,) * pps + (suffix,) * pps
        out_shape = [jax.ShapeDtypeStruct((b, n_tok, w), F32), jax.ShapeDtypeStruct((b, n_tok, N_HEADS), F32)]
        out_specs = [tok_spec, pl.BlockSpec((1, n_tok, N_HEADS), lambda i, s, pt: (i, 0, 0))]
        scratch += [pltpu.VMEM((rows, 1), F32), pltpu.VMEM((rows, 1), F32)]
    else:
        lam, norm = extras
        in_specs += [pl.BlockSpec(lam.shape, const2), pl.BlockSpec(norm.shape, const2)]
        in_specs += [page_spec(j) for j in range(pps)] * 2
        operands = (q, k_new, v_new, lam, norm) + (k_pool,) * pps + (v_pool,) * pps
        out_shape = jax.ShapeDtypeStruct((b, n_tok, w), F32)
        out_specs = tok_spec
    return pl.pallas_call(
        functools.partial(_decode_body, fox=fox, n_tok=n_tok, lam_init=lam_init),
        grid_spec=pltpu.PrefetchScalarGridSpec(
            num_scalar_prefetch=1, grid=(b, steps), in_specs=in_specs, out_specs=out_specs,
            scratch_shapes=scratch),
        out_shape=out_shape,
        compiler_params=_cparams(2),
        name="decode_fox" if fox else "decode_diff",
    )(page_table, *operands)


def _split_w_in(w):
    a0, b0, c0, d0 = 0, A_COLS, A_COLS + B_COLS, A_COLS + B_COLS + C_COLS
    main = [w[:, a0:a0 + 768], w[:, b0:b0 + 768], w[:, c0:c0 + C_CONV_CH],
            w[:, c0 + C_CONV_CH:c0 + C_CONV_CH + 256], w[:, d0:d0 + 256], w[:, d0 + 256:d0 + 256 + D_CONV_CH]]
    small = jnp.concatenate([w[:, a0 + 768:a0 + 772], w[:, c0 + 1024:c0 + 1028], w[:, c0 + 1028:c0 + 1032],
                             w[:, d0 + 768:d0 + 772]], axis=1)
    small_pad = jnp.pad(small, ((0, 0), (0, SMALL_W - small.shape[1])))
    return jnp.concatenate(main + [small_pad], axis=1).astype(BF16)


def _pages_transposed(cache):
    nd = cache.ndim
    moved = cache.transpose((0, 1) + tuple(range(3, nd)) + (2,))
    return moved.reshape(cache.shape[0], cache.shape[1], -1, PAGE_SIZE)


def _gdn_state_to_block(s):
    b = s.shape[0]
    eye = jnp.eye(N_HEADS, dtype=s.dtype)
    return jnp.einsum('bhkv,hg->bhkgv', s, eye).reshape(b, GROUP_WIDTH, GROUP_WIDTH)


def _gdn_block_to_state(sb):
    b = sb.shape[0]
    s5 = sb.reshape(b, N_HEADS, HEAD_DIM, N_HEADS, HEAD_DIM)
    return jnp.stack([s5[:, h, :, h, :] for h in range(N_HEADS)], axis=1)


def _ssd_state_to_block(s):
    b = s.shape[0]
    st = s.transpose(0, 3, 1, 2)
    grp = (jnp.arange(2)[:, None] == (jnp.arange(N_HEADS) // 2)[None, :]).astype(s.dtype)
    return jnp.einsum('bnhp,gh->bgnhp', st, grp).reshape(b, 2 * D_STATE, GROUP_WIDTH)


def _ssd_block_to_state(wb):
    b = wb.shape[0]
    w5 = wb.reshape(b, 2, D_STATE, N_HEADS, HEAD_DIM)
    s = jnp.stack([w5[:, h // 2, :, h, :] for h in range(N_HEADS)], axis=1)
    return s.transpose(0, 1, 3, 2)


def _row_tile(n):
    for tm in (768, 512, 384, 256, 128):
        if n % tm == 0:
            return tm
    return n


def _trunk(x, past, weights):
    (g_mix_pre, w_in, fox_forget_bias, diff_lambda, diff_norm, gdn_conv_w, gdn_A_log, gdn_dt_bias, gdn_norm,
     ssd_conv_w, ssd_conv_b, ssd_A_log, ssd_dt_bias, ssd_D, ssd_norm, w_out, g_mix_post, g_mlp_pre,
     w_mlp_up, w_mlp_down, g_mlp_post) = weights
    bsz, seq, d = x.shape
    n = bsz * seq
    tm = _row_tile(n)
    x2 = x.reshape(n, d)
    if past is not None:
        (cache_fox_k, cache_fox_v, cache_fox_logf, cache_diff_k, cache_diff_v,
         state_gdn_conv, state_gdn, state_ssd_conv, state_ssd, page_table) = past
        fox_k_pages, fox_v_pages = _pages_transposed(cache_fox_k), _pages_transposed(cache_fox_v)
        diff_k_pages, diff_v_pages = _pages_transposed(cache_diff_k), _pages_transposed(cache_diff_v)
        logf_rows = cache_fox_logf.transpose(0, 1, 3, 2).reshape(DEPTH, -1, PAGE_SIZE)
    layer_rows = []
    for l in range(DEPTH):
        lam_init = 0.8 - 0.6 * math.exp(-0.3 * l)
        outs = _in_proj(x2, g_mix_pre[l], _split_w_in(w_in[l]), tm)
        fq, fk, fv, dq, dk, dv, cqkv, cz, sz, sxbc, sm = [o.reshape(bsz, seq, -1) for o in outs]

        bias_lanes = jnp.concatenate([fox_forget_bias[l], gdn_dt_bias[l], jnp.zeros((4,), F32), ssd_dt_bias[l]])
        prow = jnp.zeros((SUBLANES, SMALL_W), F32).at[0, 0:bias_lanes.shape[0]].set(bias_lanes)
        lam = diff_lambda[l]
        dnorm = jnp.tile(diff_norm[l], N_HEADS).reshape(1, GROUP_WIDTH)
        gdn_par = jnp.zeros((SUBLANES, GROUP_WIDTH), F32)
        gdn_par = gdn_par.at[0].set(jnp.repeat(gdn_dt_bias[l], HEAD_DIM))
        gdn_par = gdn_par.at[1].set(jnp.repeat(gdn_A_log[l], HEAD_DIM))
        gdn_par = gdn_par.at[2].set(jnp.tile(gdn_norm[l], N_HEADS))
        ssd_par = jnp.zeros((SUBLANES, GROUP_WIDTH), F32)
        ssd_par = ssd_par.at[0].set(jnp.repeat(ssd_dt_bias[l], HEAD_DIM))
        ssd_par = ssd_par.at[1].set(jnp.repeat(ssd_A_log[l], HEAD_DIM))
        ssd_par = ssd_par.at[2].set(jnp.repeat(ssd_D[l], HEAD_DIM))
        ssd_par = ssd_par.at[3].set(ssd_norm[l])
        gdn_w8 = jnp.pad(gdn_conv_w[l], ((0, SUBLANES - CONV_W), (0, 0)))
        ssd_w8 = jnp.pad(ssd_conv_w[l], ((0, SUBLANES - CONV_W), (0, 0)))
        ssd_cb = ssd_conv_b[l].reshape(1, D_CONV_CH)
        state_pad = ((0, 0), (SUBLANES - (CONV_W - 1), 0), (0, 0))

        if past is None:
            out_a, logf = _prompt_attention(fq, fk, fv, fox=True, extras=(sm, prow))
            out_b = _prompt_attention(dq, dk, dv, fox=False, extras=(lam, dnorm), lam_init=lam_init)
            gdn_buf = jnp.zeros((bsz, SUBLANES, C_CONV_CH), F32)
            gdn_s0 = jnp.zeros((bsz, GROUP_WIDTH, GROUP_WIDTH), F32)
            ssd_buf = jnp.zeros((bsz, SUBLANES, D_CONV_CH), F32)
            ssd_s0 = jnp.zeros((bsz, 2 * D_STATE, GROUP_WIDTH), F32)
        else:
            suffix = _page_suffix_sums(logf_rows, l)
            out_a, logf = _decode_attention(page_table, fq, fk, fv, fox_k_pages, fox_v_pages, l, fox=True,
                                            extras=(sm, prow), suffix=suffix)
            out_b = _decode_attention(page_table, dq, dk, dv, diff_k_pages, diff_v_pages, l, fox=False,
                                      extras=(lam, dnorm), lam_init=lam_init)
            gdn_buf = jnp.pad(state_gdn_conv[l], state_pad)
            gdn_s0 = _gdn_state_to_block(state_gdn[l])
            ssd_buf = jnp.pad(state_ssd_conv[l], state_pad)
            ssd_s0 = _ssd_state_to_block(state_ssd[l])

        out_c, gdn_conv8, gdn_sb = _gdn(cqkv, cz, sm, gdn_buf, gdn_s0, gdn_w8, gdn_par)
        out_d, ssd_conv8, ssd_sb = _ssd(sxbc, sz, sm, ssd_buf, ssd_s0, ssd_w8, ssd_cb, ssd_par)

        mixer_outs = [o.reshape(n, GROUP_WIDTH) for o in (out_a, out_b, out_c, out_d)]
        x2 = _post(x2, mixer_outs, w_out[l].astype(BF16), g_mix_post[l], g_mlp_pre[l],
                   w_mlp_up[l].astype(BF16), w_mlp_down[l].astype(BF16), g_mlp_post[l], tm)

        tail = SUBLANES - (CONV_W - 1)
        layer_rows.append((
            fk.reshape(bsz, seq, N_HEADS, HEAD_DIM), fv.reshape(bsz, seq, N_HEADS, HEAD_DIM), logf,
            dk.reshape(bsz, seq, N_HEADS, 2, DIFF_QK), dv.reshape(bsz, seq, N_HEADS, HEAD_DIM),
            gdn_conv8[:, tail:], _gdn_block_to_state(gdn_sb),
            ssd_conv8[:, tail:], _ssd_block_to_state(ssd_sb)))
    stacked = tuple(jnp.stack([rows[i] for rows in layer_rows]) for i in range(len(layer_rows[0])))
    return x2.reshape(bsz, seq, d), stacked


def kernel(x_prompt, x_sample, cache_fox_k, cache_fox_v, cache_fox_logf, cache_diff_k, cache_diff_v, state_gdn_conv, state_gdn, state_ssd_conv, state_ssd, page_table, meta_tokens, g_mix_pre, w_in, fox_forget_bias, diff_lambda, diff_norm, gdn_conv_w, gdn_A_log, gdn_dt_bias, gdn_norm, ssd_conv_w, ssd_conv_b, ssd_A_log, ssd_dt_bias, ssd_D, ssd_norm, w_out, g_mix_post, g_mlp_pre, w_mlp_up, w_mlp_down, g_mlp_post):
    weights = (g_mix_pre, w_in, fox_forget_bias, diff_lambda, diff_norm, gdn_conv_w, gdn_A_log, gdn_dt_bias,
               gdn_norm, ssd_conv_w, ssd_conv_b, ssd_A_log, ssd_dt_bias, ssd_D, ssd_norm, w_out, g_mix_post,
               g_mlp_pre, w_mlp_up, w_mlp_down, g_mlp_post)
    bsz = x_prompt.shape[0]
    meta = jnp.broadcast_to(meta_tokens.astype(x_prompt.dtype)[None], (bsz,) + meta_tokens.shape)
    xp = jnp.concatenate([meta, x_prompt], axis=1)
    y_p, rows_p = _trunk(xp, None, weights)
    past = (cache_fox_k, cache_fox_v, cache_fox_logf, cache_diff_k, cache_diff_v,
            state_gdn_conv, state_gdn, state_ssd_conv, state_ssd, page_table)
    y_s, rows_s = _trunk(x_sample, past, weights)
    return (y_p[:, meta_tokens.shape[0]:], y_s) + rows_p + rows_s
```

```python
import functools
import math

import jax
import jax.numpy as jnp
from jax import lax
from jax.experimental import pallas as pl
from jax.experimental.pallas import tpu as pltpu

F32 = jnp.float32
BF16 = jnp.bfloat16

D_MODEL = 1024
DEPTH = 2
GROUP_WIDTH = 256
N_HEADS = 4
HEAD_DIM = 64
DIFF_QK = 32
D_STATE = 64
CONV_W = 4
C_CONV_CH = 3 * GROUP_WIDTH
D_CONV_CH = GROUP_WIDTH + 2 * 2 * D_STATE
A_COLS = 3 * GROUP_WIDTH + N_HEADS
B_COLS = 3 * GROUP_WIDTH
C_COLS = C_CONV_CH + GROUP_WIDTH + 2 * N_HEADS
D_FF = 4 * D_MODEL
PAGE_SIZE = 128
EPS = 1e-6
NEG = -1e30

LANES = 128
SUBLANES = 8
SMALL_W = 128
ATTN_Q_TILE = 128
ATTN_K_BLOCK = 256
GDN_CHUNK = 64
GDN_PRE_TILE = 256
SSD_CHUNK = 128
PAGES_PER_STEP = 16
SUFFIX_ROWS = 256
VMEM_LIMIT = 56 * 1024 * 1024

PROJ_WIDTHS = (256, 256, 256, 256, 256, 256, C_CONV_CH, 256, 256, D_CONV_CH, SMALL_W)


def _softplus(x):
    return jnp.maximum(x, 0.0) + jnp.log1p(jnp.exp(-jnp.abs(x)))


def _sigmoid(x):
    return 1.0 / (1.0 + jnp.exp(-x))


def _silu(x):
    return x * _sigmoid(x)


def _dot(a, b, prec=None):
    return jnp.dot(a, b, preferred_element_type=F32, precision=prec)


def _dot_nt(a, b, prec=None):
    return lax.dot_general(a, b, (((1,), (1,)), ((), ())), preferred_element_type=F32, precision=prec)


def _dot_tn(a, b, prec=None):
    return lax.dot_general(a, b, (((0,), (0,)), ((), ())), preferred_element_type=F32, precision=prec)


def _bf16_terms(x, terms):
    pieces = []
    r = x
    for t in range(terms):
        p = r.astype(BF16)
        pieces.append(p)
        if t + 1 < terms:
            r = r - p.astype(F32)
    return pieces


def _dot_split(x, w, terms, dot=_dot):
    w_bf = w.astype(BF16)
    acc = None
    for p in _bf16_terms(x, terms):
        part = dot(p, w_bf)
        acc = part if acc is None else acc + part
    return acc


def _dot_split_rhs(w, x, terms, dot=_dot):
    w_bf = w.astype(BF16)
    acc = None
    for p in _bf16_terms(x, terms):
        part = dot(w_bf, p)
        acc = part if acc is None else acc + part
    return acc


def _mm1(a, b):
    return _dot(a.astype(BF16), b.astype(BF16))


def _iota(shape, dim):
    return lax.broadcasted_iota(jnp.int32, shape, dim)


def _lane_group_mask(width, group, idx):
    return (_iota((1, width), 1) // group) == idx


def _block_ones(n, group):
    return ((_iota((n, n), 0) // group) == (_iota((n, n), 1) // group)).astype(F32)


def _lower_tri(n, strict=False):
    ri, ci = _iota((n, n), 0), _iota((n, n), 1)
    return (ri > ci) if strict else (ri >= ci)


def _stack_masked(x, group, groups):
    zero = jnp.zeros_like(x)
    return jnp.concatenate([jnp.where(_lane_group_mask(x.shape[1], group, g), x, zero) for g in groups], axis=0)


def _unstack_sum(xs, rows, n_groups):
    out = xs[0:rows]
    for g in range(1, n_groups):
        out = out + xs[g * rows:(g + 1) * rows]
    return out


def _unstack_select(xs, rows, group, blocks):
    out = None
    for g, blk in enumerate(blocks):
        part = jnp.where(_lane_group_mask(xs.shape[1], group, g), xs[blk * rows:(blk + 1) * rows], 0.0)
        out = part if out is None else out + part
    return out


def _pad_rows(x, rows):
    if x.shape[0] == rows:
        return x
    return jnp.concatenate([x, jnp.zeros((rows - x.shape[0], x.shape[1]), x.dtype)], axis=0)


def _cparams(n_grid_dims):
    return pltpu.CompilerParams(dimension_semantics=("arbitrary",) * n_grid_dims,
                                vmem_limit_bytes=VMEM_LIMIT)


def _in_proj_body(x_ref, g_ref, w_ref, *out_refs):
    x = x_ref[...]
    ms = jnp.mean(x * x, axis=-1, keepdims=True)
    h = (x * lax.rsqrt(ms + EPS) * g_ref[...]).astype(BF16)
    off = 0
    for o_ref, width in zip(out_refs, PROJ_WIDTHS):
        o_ref[...] = _dot(h, w_ref[:, off:off + width])
        off += width


def _in_proj(x2d, g, w_main, tm):
    n, d = x2d.shape
    assert n % tm == 0
    return pl.pallas_call(
        _in_proj_body,
        grid=(n // tm,),
        in_specs=[pl.BlockSpec((tm, d), lambda i: (i, 0)),
                  pl.BlockSpec((1, d), lambda i: (0, 0)),
                  pl.BlockSpec(w_main.shape, lambda i: (0, 0))],
        out_specs=[pl.BlockSpec((tm, w), lambda i: (i, 0)) for w in PROJ_WIDTHS],
        out_shape=[jax.ShapeDtypeStruct((n, w), F32) for w in PROJ_WIDTHS],
        compiler_params=_cparams(1),
        name="in_proj",
    )(x2d, g.reshape(1, d), w_main)


def _post_body(x_ref, oa_ref, ob_ref, oc_ref, od_ref, wo_ref, gpost_ref, gpre_ref, wu_ref, wd_ref,
               gmlp_ref, y_ref, *, ff_chunk):
    mix = None
    for i, o_ref in enumerate((oa_ref, ob_ref, oc_ref, od_ref)):
        part = _dot(o_ref[...].astype(BF16), wo_ref[i * GROUP_WIDTH:(i + 1) * GROUP_WIDTH, :])
        mix = part if mix is None else mix + part
    ms = jnp.mean(mix * mix, axis=-1, keepdims=True)
    x1 = x_ref[...] + mix * lax.rsqrt(ms + EPS) * gpost_ref[...]
    ms = jnp.mean(x1 * x1, axis=-1, keepdims=True)
    h = (x1 * lax.rsqrt(ms + EPS) * gpre_ref[...]).astype(BF16)
    acc = None
    for c in range(D_FF // ff_chunk):
        u = jnp.maximum(_dot(h, wu_ref[:, c * ff_chunk:(c + 1) * ff_chunk]), 0.0)
        part = _dot((u * u).astype(BF16), wd_ref[c * ff_chunk:(c + 1) * ff_chunk, :])
        acc = part if acc is None else acc + part
    ms = jnp.mean(acc * acc, axis=-1, keepdims=True)
    y_ref[...] = x1 + acc * lax.rsqrt(ms + EPS) * gmlp_ref[...]


def _post(x2d, outs, w_out, g_post, g_pre, w_up, w_down, g_mlp, tm):
    n, d = x2d.shape
    assert n % tm == 0
    row = lambda i: (i, 0)
    const = lambda i: (0, 0)
    single = pl.Buffered(1)
    return pl.pallas_call(
        functools.partial(_post_body, ff_chunk=512),
        grid=(n // tm,),
        in_specs=[pl.BlockSpec((tm, d), row)]
        + [pl.BlockSpec((tm, GROUP_WIDTH), row)] * 4
        + [pl.BlockSpec(w_out.shape, const, pipeline_mode=single),
           pl.BlockSpec((1, d), const), pl.BlockSpec((1, d), const),
           pl.BlockSpec(w_up.shape, const, pipeline_mode=single),
           pl.BlockSpec(w_down.shape, const, pipeline_mode=single),
           pl.BlockSpec((1, d), const)],
        out_specs=pl.BlockSpec((tm, d), row),
        out_shape=jax.ShapeDtypeStruct((n, d), F32),
        compiler_params=_cparams(1),
        name="post_mlp",
    )(x2d, *outs, w_out, g_post.reshape(1, d), g_pre.reshape(1, d), w_up, w_down, g_mlp.reshape(1, d))


def _diff_lambda(lam_ref, lam_init):
    lp = lam_ref[...]
    return (jnp.exp(jnp.sum(lp[0:1] * lp[1:2], axis=-1, keepdims=True))
            - jnp.exp(jnp.sum(lp[2:3] * lp[3:4], axis=-1, keepdims=True)) + lam_init)


def _stack_rows(x, size):
    return jnp.concatenate([x[:, h:h + 1] for h in range(N_HEADS)], axis=0)


def _stack_lanes(x, size):
    return jnp.concatenate([jnp.broadcast_to(x[h:h + 1, :], (size, x.shape[1])) for h in range(N_HEADS)],
                           axis=0)


def _prompt_attn_body(*refs, fox, seq, lam_init):
    if fox:
        (q_ref, k_ref, v_ref, sm_ref, prow_ref, o_ref, logf_ref,
         kb_s, qt_s, vt_s, m_s, l_s, acc_s, ckx_s) = refs
        groups = list(range(N_HEADS))
        group = HEAD_DIM
    else:
        (q_ref, k_ref, v_ref, lam_ref, norm_ref, o_ref, kb_s, qt_s, vt_s, m_s, l_s, acc_s) = refs
        groups = [2 * h for h in range(N_HEADS)] + [2 * h + 1 for h in range(N_HEADS)]
        group = DIFF_QK
    tq, tk = ATTN_Q_TILE, ATTN_K_BLOCK
    n_q = -(-seq // tq)
    seq_pad = kb_s.shape[0]
    last_valid = seq - (n_q - 1) * tq
    n_blocks = len(groups)
    width = n_blocks * tq

    kb_s[0:seq, :] = k_ref[0].astype(BF16)
    kb_s[seq:seq_pad, :] = jnp.zeros((seq_pad - seq, GROUP_WIDTH), BF16)
    if seq_pad > n_q * tq:
        zeros = jnp.zeros((GROUP_WIDTH, seq_pad - n_q * tq), BF16)
        qt_s[:, n_q * tq:seq_pad] = zeros
        vt_s[:, n_q * tq:seq_pad] = zeros
    if fox:
        if seq_pad > n_q * tq:
            ckx_s[n_q * tq:seq_pad, :] = jnp.zeros((seq_pad - n_q * tq, width), F32)
        bias_row = prow_ref[0:1, :]
        tri_lo = _lower_tri(tq).astype(F32)

    def stage_tile(start, valid, carry):
        qt_s[:, pl.ds(start, tq)] = (_pad_rows(q_ref[0, pl.ds(start, valid), :], tq)
                                     * (group ** -0.5)).T.astype(BF16)
        vt_s[:, pl.ds(start, tq)] = _pad_rows(v_ref[0, pl.ds(start, valid), :], tq).T.astype(BF16)
        if not fox:
            return carry
        lf = -_softplus(-(_pad_rows(sm_ref[0, pl.ds(start, valid), :], tq) + bias_row))
        logf_ref[0, pl.ds(start, valid), :] = lf[0:valid, 0:N_HEADS]
        cc = _dot_split_rhs(tri_lo, lf, 3) + carry
        ckx_s[pl.ds(start, tq), :] = jnp.concatenate(
            [jnp.broadcast_to(cc[:, h:h + 1], (tq, tq)) for h in range(N_HEADS)], axis=1)
        return cc[tq - 1:tq, :]

    carry = jnp.zeros((1, SMALL_W), F32)
    carry = lax.fori_loop(0, n_q - 1, lambda i, c: stage_tile(pl.multiple_of(i * tq, tq), tq, c), carry)
    stage_tile((n_q - 1) * tq, last_valid, carry)

    row_group = _iota((GROUP_WIDTH, 1), 0) // group
    lane_q = _iota((1, width), 1) % tq
    key_off = _iota((tk, 1), 0)

    def q_tile(idx, valid):
        static = isinstance(idx, int)
        qs0 = idx * tq if static else pl.multiple_of(idx * tq, tq)
        qt = qt_s[:, pl.ds(qs0, tq)]
        zero = jnp.zeros_like(qt)
        qstack = jnp.concatenate([jnp.where(row_group == g, qt, zero) for g in groups], axis=1)
        m_s[...] = jnp.full(m_s.shape, NEG, F32)
        l_s[...] = jnp.zeros(l_s.shape, F32)
        acc_s[...] = jnp.zeros(acc_s.shape, F32)

        def scores(kb0, masked):
            s = _dot(kb_s[pl.ds(kb0, tk), :], qstack)
            if fox:
                s = s - ckx_s[pl.ds(kb0, tk), :]
            if masked:
                s = jnp.where((kb0 + key_off) <= (qs0 + lane_q), s, NEG)
            return s

        def absorb(s, kb0):
            m_old = m_s[...]
            m_new = jnp.maximum(m_old, jnp.max(s, axis=0, keepdims=True))
            alpha = jnp.exp(m_old - m_new)
            p = jnp.exp(s - m_new)
            l_s[...] = alpha * l_s[...] + jnp.sum(p, axis=0, keepdims=True)
            m_s[...] = m_new
            pb = p.astype(BF16)
            for b in range(n_blocks):
                h = b % N_HEADS
                upd = _dot(vt_s[h * HEAD_DIM:(h + 1) * HEAD_DIM, pl.ds(kb0, tk)], pb[:, b * tq:(b + 1) * tq])
                rows = slice(b * HEAD_DIM, (b + 1) * HEAD_DIM)
                acc_s[rows, :] = acc_s[rows, :] * alpha[:, b * tq:(b + 1) * tq] + upd

        def pair(kb_a, kb_b, mask_b):
            sa = scores(kb_a, False)
            sb = scores(kb_b, mask_b)
            absorb(sa, kb_a)
            absorb(sb, kb_b)

        def single(kb0):
            absorb(scores(kb0, True), kb0)

        n_before = (idx * tq) // tk

        def pair_step(j, c):
            kb = pl.multiple_of(2 * j * tk, 2 * tk)
            pair(kb, pl.multiple_of(kb + tk, tk), False)
            return c

        lax.fori_loop(0, n_before // 2, pair_step, 0)
        if static:
            last = n_before * tk
            if n_before % 2:
                pair(last - tk, last, True)
            else:
                single(last)
        else:
            last = pl.multiple_of(n_before * tk, tk)
            pl.when(n_before % 2 == 1)(lambda: pair(pl.multiple_of(last - tk, tk), last, True))
            pl.when(n_before % 2 == 0)(lambda: single(last))

        l = l_s[...]
        o_t = jnp.concatenate([acc_s[b * HEAD_DIM:(b + 1) * HEAD_DIM, :] / l[:, b * tq:(b + 1) * tq]
                               for b in range(n_blocks)], axis=0)
        if fox:
            o = o_t.T
        else:
            od = o_t[0:GROUP_WIDTH] - _diff_lambda(lam_ref, lam_init) * o_t[GROUP_WIDTH:]
            normed = []
            for h in range(N_HEADS):
                blk = od[h * HEAD_DIM:(h + 1) * HEAD_DIM]
                ms = jnp.mean(blk * blk, axis=0, keepdims=True)
                normed.append(blk * lax.rsqrt(ms + EPS))
            o = jnp.concatenate(normed, axis=0).T * norm_ref[...] * (1.0 - lam_init)
        o_ref[0, pl.ds(qs0, valid), :] = o[0:valid]

    def loop_body(i, c):
        q_tile(i, tq)
        return c

    lax.fori_loop(0, n_q - 1, loop_body, 0)
    q_tile(n_q - 1, last_valid)


def _prompt_attention(q, k, v, *, fox, extras, lam_init=0.0):
    b, seq, w = q.shape
    tq, tk = ATTN_Q_TILE, ATTN_K_BLOCK
    n_q = -(-seq // tq)
    seq_pad = -(-(n_q * tq) // tk) * tk
    n_blocks = N_HEADS if fox else 2 * N_HEADS
    seq_spec = pl.BlockSpec((1, seq, w), lambda i: (i, 0, 0))
    const2 = lambda i: (0, 0)
    scratch = [pltpu.VMEM((seq_pad, w), BF16), pltpu.VMEM((w, seq_pad), BF16), pltpu.VMEM((w, seq_pad), BF16),
               pltpu.VMEM((1, n_blocks * tq), F32), pltpu.VMEM((1, n_blocks * tq), F32),
               pltpu.VMEM((n_blocks * HEAD_DIM, tq), F32)]
    if fox:
        sm, prow = extras
        in_specs = [seq_spec] * 3 + [pl.BlockSpec((1, seq, SMALL_W), lambda i: (i, 0, 0)),
                                     pl.BlockSpec(prow.shape, const2)]
        out_shape = [jax.ShapeDtypeStruct((b, seq, w), F32), jax.ShapeDtypeStruct((b, seq, N_HEADS), F32)]
        out_specs = [seq_spec, pl.BlockSpec((1, seq, N_HEADS), lambda i: (i, 0, 0))]
        scratch = scratch + [pltpu.VMEM((seq_pad, n_blocks * tq), F32)]
    else:
        lam, norm = extras
        in_specs = [seq_spec] * 3 + [pl.BlockSpec(lam.shape, const2), pl.BlockSpec(norm.shape, const2)]
        out_shape = jax.ShapeDtypeStruct((b, seq, w), F32)
        out_specs = seq_spec
    return pl.pallas_call(
        functools.partial(_prompt_attn_body, fox=fox, seq=seq, lam_init=lam_init),
        grid=(b,),
        in_specs=in_specs,
        out_specs=out_specs,
        out_shape=out_shape,
        scratch_shapes=scratch,
        compiler_params=_cparams(1),
        name="prompt_fox" if fox else "prompt_diff",
    )(q, k, v, *extras)


def _fill_conv_scratch(x_ref, buf_ref, xpad_s, new_ref, seq):
    xpad_s[0:SUBLANES, :] = buf_ref[0]
    xpad_s[SUBLANES:SUBLANES + seq, :] = x_ref[0]
    tail = xpad_s.shape[0] - SUBLANES - seq
    if tail:
        xpad_s[SUBLANES + seq:, :] = jnp.zeros((tail, xpad_s.shape[1]), F32)
    new_ref[0] = xpad_s[seq:seq + SUBLANES, :]


def _conv_chunk(xpad_s, w_ref, start, size):
    n = size + SUBLANES
    win = xpad_s[pl.ds(start, n), :]
    out = None
    for i in range(CONV_W):
        first = SUBLANES - (CONV_W - 1) + i
        part = pltpu.roll(win, n - first, axis=0)[0:size] * w_ref[i:i + 1, :]
        out = part if out is None else out + part
    return out


def _head_selector(first_lanes):
    parts = [(_iota((SMALL_W, GROUP_WIDTH), 0) == first + _iota((SMALL_W, GROUP_WIDTH), 1) // HEAD_DIM)
             for first in first_lanes]
    return jnp.concatenate(parts, axis=1).astype(F32)


def _expand_heads(sm, first_lane):
    return _dot_split(sm, _head_selector([first_lane]), 3)


def _head_rows(x_exp):
    sel = (_iota((SUBLANES, GROUP_WIDTH), 0) == _iota((SUBLANES, GROUP_WIDTH), 1) // HEAD_DIM)
    return _dot_split_rhs(sel.astype(F32) * (1.0 / HEAD_DIM), x_exp, 3, _dot_nt)


def _inv_unit_lower(mats, n, blk):
    ri = _iota((n, n), 0)
    ci = _iota((n, n), 1)
    eye = (ri == ci).astype(F32)
    base = min(16, blk)
    in_base = (ri // base) == (ci // base)
    ps = [-jnp.where(in_base, a, 0.0) for a in mats]
    xs = [eye + p for p in ps]
    s = 1
    while 2 * s < base:
        ps = [_mm1(p, p) for p in ps]
        xs = [x + _mm1(x, p) for x, p in zip(xs, ps)]
        s *= 2
    size = base
    while size < blk:
        off_diag = ((ri // (2 * size)) == (ci // (2 * size))) & ((ri // size) != (ci // size))
        ts = [_mm1(x, jnp.where(off_diag, a, 0.0)) for x, a in zip(xs, mats)]
        xs = [x - _mm1(t, x) for x, t in zip(xs, ts)]
        size *= 2
    return xs


def _tile_schedule(seq, tile, fn):
    n_tiles = -(-seq // tile)
    if n_tiles > 1:
        def body(i, c):
            fn(pl.multiple_of(i * tile, tile), tile)
            return c
        lax.fori_loop(0, n_tiles - 1, body, 0)
    fn((n_tiles - 1) * tile, seq - (n_tiles - 1) * tile)


def _gdn_body(x_ref, z_ref, sm_ref, buf_ref, s0_ref, w_ref, par_ref,
              o_ref, new_ref, sn_ref, xpad_s, u_s, w_s, qe_s, kd_s, pm_s, eg_s, o_s, st_s, *, seq, pre_tile):
    _fill_conv_scratch(x_ref, buf_ref, xpad_s, new_ref, seq)
    st_s[...] = s0_ref[0]
    size = GDN_CHUNK
    n = N_HEADS * size
    dtb = par_ref[0:1, :]
    a_neg = -jnp.exp(par_ref[1:2, :])
    gain = par_ref[2:3, :]
    head_ones = _block_ones(GROUP_WIDTH, HEAD_DIM)
    heads = range(N_HEADS)
    ri = _iota((n, n), 0)
    ci = _iota((n, n), 1)
    same = (ri // size) == (ci // size)
    tri = _lower_tri(size).astype(F32)
    mean_row = jnp.full((SUBLANES, GROUP_WIDTH), 1.0 / HEAD_DIM, F32)

    def prepare(start, valid):
        act = _silu(_conv_chunk(xpad_s, w_ref, start, pre_tile))
        q = act[:, 0:GROUP_WIDTH]
        k = act[:, GROUP_WIDTH:2 * GROUP_WIDTH]
        v = act[:, 2 * GROUP_WIDTH:3 * GROUP_WIDTH]
        q = q * lax.rsqrt(_dot_split(q * q, head_ones, 2) + EPS) * (HEAD_DIM ** -0.5)
        k = k * lax.rsqrt(_dot_split(k * k, head_ones, 2) + EPS)
        ab = _dot_split(_pad_rows(sm_ref[0, pl.ds(start, valid), :], pre_tile), _head_selector([4, 8]), 3)
        g = a_neg * _softplus(ab[:, 0:GROUP_WIDTH] + dtb)
        beta = _sigmoid(ab[:, GROUP_WIDTH:])
        if valid < pre_tile:
            live = _iota((pre_tile, 1), 0) < valid
            g = jnp.where(live, g, 0.0)
            beta = jnp.where(live, beta, 0.0)
        first_chunk = start // size if isinstance(start, int) else lax.div(start, size)
        chunks = range(pre_tile // size)
        part = lambda x: [x[c * size:(c + 1) * size] for c in chunks]
        qc, kc, vc, bc = part(q), part(k), part(v), part(beta)
        gam = [_dot_split_rhs(tri, gi, 3) for gi in part(g)]
        egam = [jnp.exp(x) for x in gam]
        gcol = [jnp.concatenate([x[:, h * HEAD_DIM:h * HEAD_DIM + 1] for h in heads], axis=0) for x in gam]
        grow = [_dot_split_rhs(mean_row, _stack_masked(x, HEAD_DIM, heads), 3, _dot_nt)[0:1] for x in gam]
        bcol = [jnp.concatenate([x[:, h * HEAD_DIM:h * HEAD_DIM + 1] for h in heads], axis=0) for x in bc]
        decay = [jnp.exp(jnp.where(same & (ri >= ci), a - b, -jnp.inf)) for a, b in zip(gcol, grow)]
        ks = [_stack_masked(x, HEAD_DIM, heads).astype(BF16) for x in kc]
        qs = [_stack_masked(x, HEAD_DIM, heads).astype(BF16) for x in qc]
        kk = [_dot_nt(x, x) for x in ks]
        a_mat = [jnp.where(same & (ri > ci), x * d, 0.0) * b for x, d, b in zip(kk, decay, bcol)]
        t_inv = _inv_unit_lower(a_mat, n, size)
        rhs = [jnp.concatenate([_stack_masked(x * b, HEAD_DIM, heads),
                                _stack_masked(y * e * b, HEAD_DIM, heads)], axis=1)
               for x, y, e, b in zip(vc, kc, egam, bc)]
        uw = [_mm1(t, r) for t, r in zip(t_inv, rhs)]
        qk = [_dot_nt(x, y) for x, y in zip(qs, ks)]
        for c in chunks:
            rows = pl.ds(start + c * size, size)
            u_s[rows, :] = _unstack_sum(uw[c][:, 0:GROUP_WIDTH], size, N_HEADS)
            w_s[rows, :] = _unstack_sum(uw[c][:, GROUP_WIDTH:], size, N_HEADS).astype(BF16)
            qe_s[rows, :] = (qc[c] * egam[c]).astype(BF16)
            g_last = gam[c][size - 1:size, :]
            kd_s[rows, :] = (kc[c] * jnp.exp(g_last - gam[c])).astype(BF16)
            idx = first_chunk + c
            pm_s[idx] = (qk[c] * decay[c]).astype(BF16)
            eg_start = idx * SUBLANES if isinstance(idx, int) else pl.multiple_of(idx * SUBLANES, SUBLANES)
            eg_s[pl.ds(eg_start, SUBLANES), :] = jnp.broadcast_to(jnp.exp(g_last), (SUBLANES, GROUP_WIDTH))

    _tile_schedule(seq, pre_tile, prepare)

    def scan(i, carry):
        rows = pl.ds(pl.multiple_of(i * size, size), size)
        state = st_s[...]
        ws = _dot(jnp.concatenate([w_s[rows, :], qe_s[rows, :]], axis=0), state.astype(BF16))
        v_new = u_s[rows, :] - ws[0:size]
        intra = _dot(pm_s[i], _stack_masked(v_new, HEAD_DIM, heads).astype(BF16))
        o_s[rows, :] = _unstack_sum(intra, size, N_HEADS) + ws[size:]
        eg = eg_s[pl.ds(pl.multiple_of(i * SUBLANES, SUBLANES), 1), :]
        st_s[...] = state * eg + _dot_tn(kd_s[rows, :], v_new.astype(BF16)) * head_ones
        return carry

    lax.fori_loop(0, -(-seq // size), scan, 0)
    sn_ref[0] = st_s[...]

    def finish(start, valid):
        o = o_s[pl.ds(start, pre_tile), :]
        ms = _dot_split(o * o, head_ones, 2) * (1.0 / HEAD_DIM)
        y = o * lax.rsqrt(ms + EPS) * gain
        o_ref[0, pl.ds(start, valid), :] = y[0:valid] * _silu(z_ref[0, pl.ds(start, valid), :])

    _tile_schedule(seq, pre_tile, finish)


def _gdn(x, z, sm, buf8, s0, conv_w8, par):
    b, seq, ch = x.shape
    pre_tile = GDN_PRE_TILE if seq > GDN_CHUNK else GDN_CHUNK
    rows_pad = -(-seq // pre_tile) * pre_tile
    n_chunks = rows_pad // GDN_CHUNK
    bspec = lambda shape: pl.BlockSpec((1,) + shape, lambda i: (i, 0, 0))
    const2 = lambda i: (0, 0)
    return pl.pallas_call(
        functools.partial(_gdn_body, seq=seq, pre_tile=pre_tile),
        grid=(b,),
        in_specs=[bspec((seq, ch)), bspec((seq, GROUP_WIDTH)), bspec((seq, SMALL_W)), bspec((SUBLANES, ch)),
                  bspec((GROUP_WIDTH, GROUP_WIDTH)), pl.BlockSpec(conv_w8.shape, const2),
                  pl.BlockSpec(par.shape, const2)],
        out_specs=[bspec((seq, GROUP_WIDTH)), bspec((SUBLANES, ch)), bspec((GROUP_WIDTH, GROUP_WIDTH))],
        out_shape=[jax.ShapeDtypeStruct((b, seq, GROUP_WIDTH), F32),
                   jax.ShapeDtypeStruct((b, SUBLANES, ch), F32),
                   jax.ShapeDtypeStruct((b, GROUP_WIDTH, GROUP_WIDTH), F32)],
        scratch_shapes=[pltpu.VMEM((rows_pad + 2 * SUBLANES, ch), F32),
                        pltpu.VMEM((rows_pad, GROUP_WIDTH), F32),
                        pltpu.VMEM((rows_pad, GROUP_WIDTH), BF16),
                        pltpu.VMEM((rows_pad, GROUP_WIDTH), BF16),
                        pltpu.VMEM((rows_pad, GROUP_WIDTH), BF16),
                        pltpu.VMEM((n_chunks, N_HEADS * GDN_CHUNK, N_HEADS * GDN_CHUNK), BF16),
                        pltpu.VMEM((n_chunks * SUBLANES, GROUP_WIDTH), F32),
                        pltpu.VMEM((rows_pad, GROUP_WIDTH), F32),
                        pltpu.VMEM((GROUP_WIDTH, GROUP_WIDTH), F32)],
        compiler_params=_cparams(1),
        name="gdn",
    )(x, z, sm, buf8, s0, conv_w8, par)


def _ssd_body(x_ref, z_ref, sm_ref, buf_ref, s0_ref, w_ref, cb_ref, par_ref,
              o_ref, new_ref, sn_ref, xpad_s, st_s, *, seq):
    _fill_conv_scratch(x_ref, buf_ref, xpad_s, new_ref, seq)
    st_s[...] = s0_ref[0]
    size = SSD_CHUNK
    dtb = par_ref[0:1, :]
    a_neg = -jnp.exp(par_ref[1:2, :])
    d_skip = par_ref[2:3, :]
    gain = par_ref[3:4, :]
    n_state = 2 * D_STATE
    live_state = ((_iota((n_state, GROUP_WIDTH), 0) // D_STATE)
                  == (_iota((n_state, GROUP_WIDTH), 1) // (2 * HEAD_DIM))).astype(F32)

    def chunk(start, valid):
        pre = _silu(_conv_chunk(xpad_s, w_ref, start, size) + cb_ref[...])
        sx = pre[:, 0:GROUP_WIDTH]
        sb = pre[:, GROUP_WIDTH:GROUP_WIDTH + n_state]
        sc = pre[:, GROUP_WIDTH + n_state:]
        sm = _pad_rows(sm_ref[0, pl.ds(start, valid), :], size)
        dt = _softplus(_expand_heads(sm, 12) + dtb)
        if valid < size:
            dt = jnp.where(_iota((size, 1), 0) < valid, dt, 0.0)
        a = dt * a_neg
        tri = _lower_tri(size)
        cum = _dot_split_rhs(tri.astype(F32), a, 3)
        ecum = jnp.exp(cum)
        cum_rows = _head_rows(cum)
        xdt = sx * dt
        sb_bf = sb.astype(BF16)
        cb = [_dot_nt(jnp.where(_lane_group_mask(n_state, D_STATE, g), sc, 0.0).astype(BF16), sb_bf)
              for g in range(2)]
        y = None
        for h in range(N_HEADS):
            dec = jnp.exp(jnp.where(tri, cum[:, h * HEAD_DIM:h * HEAD_DIM + 1] - cum_rows[h:h + 1, :], -jnp.inf))
            xh = jnp.where(_lane_group_mask(GROUP_WIDTH, HEAD_DIM, h), xdt, 0.0).astype(BF16)
            part = _dot((cb[h // 2] * dec).astype(BF16), xh)
            y = part if y is None else y + part
        state = st_s[...]
        y = y + _dot(sc.astype(BF16), state.astype(BF16)) * ecum
        c_last = cum[size - 1:size, :]
        st_s[...] = (state * jnp.exp(c_last)
                     + _dot_tn(sb_bf, (xdt * jnp.exp(c_last - cum)).astype(BF16)) * live_state)
        ys = (y + d_skip * sx)[0:valid] * _silu(z_ref[0, pl.ds(start, valid), :])
        ms = jnp.mean(ys * ys, axis=-1, keepdims=True)
        o_ref[0, pl.ds(start, valid), :] = ys * lax.rsqrt(ms + EPS) * gain

    _tile_schedule(seq, size, chunk)
    sn_ref[0] = st_s[...]


def _ssd(x, z, sm, buf8, s0, conv_w8, conv_b, par):
    b, seq, ch = x.shape
    n_state = 2 * D_STATE
    seq_pad = -(-seq // SSD_CHUNK) * SSD_CHUNK
    bspec = lambda shape: pl.BlockSpec((1,) + shape, lambda i: (i, 0, 0))
    const2 = lambda i: (0, 0)
    return pl.pallas_call(
        functools.partial(_ssd_body, seq=seq),
        grid=(b,),
        in_specs=[bspec((seq, ch)), bspec((seq, GROUP_WIDTH)), bspec((seq, SMALL_W)), bspec((SUBLANES, ch)),
                  bspec((n_state, GROUP_WIDTH)), pl.BlockSpec(conv_w8.shape, const2),
                  pl.BlockSpec(conv_b.shape, const2), pl.BlockSpec(par.shape, const2)],
        out_specs=[bspec((seq, GROUP_WIDTH)), bspec((SUBLANES, ch)), bspec((n_state, GROUP_WIDTH))],
        out_shape=[jax.ShapeDtypeStruct((b, seq, GROUP_WIDTH), F32),
                   jax.ShapeDtypeStruct((b, SUBLANES, ch), F32),
                   jax.ShapeDtypeStruct((b, n_state, GROUP_WIDTH), F32)],
        scratch_shapes=[pltpu.VMEM((seq_pad + 2 * SUBLANES, ch), F32), pltpu.VMEM((n_state, GROUP_WIDTH), F32)],
        compiler_params=_cparams(1),
        name="ssd",
    )(x, z, sm, buf8, s0, conv_w8, conv_b, par)


def _suffix_body(x_ref, o_ref):
    later = (_iota((PAGE_SIZE, PAGE_SIZE), 0) >= _iota((PAGE_SIZE, PAGE_SIZE), 1)).astype(F32)
    o_ref[0] = _dot_split(x_ref[0], later, 3)


def _page_suffix_sums(logf_rows, layer):
    n_rows = logf_rows.shape[1]
    rows = SUFFIX_ROWS if n_rows % SUFFIX_ROWS == 0 else n_rows
    out = pl.pallas_call(
        _suffix_body,
        grid=(n_rows // rows,),
        in_specs=[pl.BlockSpec((1, rows, PAGE_SIZE), lambda i: (layer, i, 0))],
        out_specs=pl.BlockSpec((1, rows, PAGE_SIZE), lambda i: (0, i, 0)),
        out_shape=jax.ShapeDtypeStruct((1, n_rows, PAGE_SIZE), F32),
        compiler_params=_cparams(1),
        name="page_suffix",
    )(logf_rows)
    return out.reshape(n_rows // N_HEADS, N_HEADS, PAGE_SIZE)


def _decode_body(pt_ref, *refs, fox, n_tok, lam_init):
    pps = PAGES_PER_STEP
    if fox:
        q_ref, kn_ref, vn_ref, sm_ref, prow_ref = refs[:5]
        rest = refs[5:]
        k_refs, v_refs, r_refs = rest[:pps], rest[pps:2 * pps], rest[2 * pps:3 * pps]
        o_ref, logf_ref, qs_s, m_s, l_s, acc_s, rq_s, carry_s = rest[3 * pps:]
        groups = list(range(N_HEADS))
        group = HEAD_DIM
    else:
        q_ref, kn_ref, vn_ref, lam_ref, norm_ref = refs[:5]
        rest = refs[5:]
        k_refs, v_refs = rest[:pps], rest[pps:2 * pps]
        o_ref, qs_s, m_s, l_s, acc_s = rest[2 * pps:]
        groups = [2 * h for h in range(N_HEADS)] + [2 * h + 1 for h in range(N_HEADS)]
        group = DIFF_QK
    step = pl.program_id(1)
    rows = len(groups) * n_tok
    t = PAGE_SIZE

    def update(scores, value_products):
        m_old = m_s[...]
        top = scores[0]
        for s in scores[1:]:
            top = jnp.maximum(top, s)
        m_new = jnp.maximum(m_old, jnp.max(top, axis=-1, keepdims=True))
        alpha = jnp.exp(m_old - m_new)
        probs = [jnp.exp(s - m_new) for s in scores]
        total = probs[0]
        for p in probs[1:]:
            total = total + p
        l_s[...] = alpha * l_s[...] + jnp.sum(total, axis=-1, keepdims=True)
        pv = None
        for p, product in zip(probs, value_products):
            part = product(p.astype(BF16))
            pv = part if pv is None else pv + part
        acc_s[...] = alpha * acc_s[...] + pv
        m_s[...] = m_new

    @pl.when(step == 0)
    def _init():
        qs_s[...] = _stack_masked((q_ref[0] * (group ** -0.5)).astype(BF16), group, groups)
        m_s[...] = jnp.full((rows, 1), NEG, F32)
        l_s[...] = jnp.zeros((rows, 1), F32)
        acc_s[...] = jnp.zeros((rows, GROUP_WIDTH), F32)
        col = _iota((rows, t), 1)
        mask = (col < n_tok) & ((_iota((rows, t), 0) % n_tok) >= col)
        s = _dot_nt(qs_s[...], _pad_rows(kn_ref[0].astype(BF16), t))
        if fox:
            lf = -_softplus(-(sm_ref[0] + prow_ref[0:1, :]))
            logf_ref[0] = lf[:, 0:N_HEADS]
            later = (_iota((n_tok, n_tok), 0) < _iota((n_tok, n_tok), 1)).astype(F32)
            after = _dot_split_rhs(later, lf, 3)
            head_sel = (_iota((SUBLANES, LANES), 0) == _iota((SUBLANES, LANES), 1)).astype(F32)
            after_rows = _dot_split_rhs(head_sel, _pad_rows(after, t), 3, _dot_nt)
            rq = _stack_rows(after, n_tok)
            rq_s[...] = rq
            carry_s[...] = _stack_rows(jnp.broadcast_to(jnp.sum(lf, axis=0, keepdims=True), (n_tok, SMALL_W)),
                                       n_tok)
            s = s + (_stack_lanes(after_rows, n_tok) - rq)
        v_new = _pad_rows(vn_ref[0].astype(BF16), t)
        update([jnp.where(mask, s, NEG)], [lambda p: _dot(p, v_new)])

    qs = qs_s[...]
    scores = []
    products = []
    if fox:
        carry = carry_s[...]
        rq = rq_s[...]
    for j in range(pps):
        s = _dot(qs, k_refs[j][0, 0].astype(BF16))
        if fox:
            incl = _stack_lanes(r_refs[j][0], n_tok)
            excl = jnp.where(_iota((rows, t), 1) == t - 1, 0.0, pltpu.roll(incl, t - 1, axis=1))
            s = s + (excl + (carry - rq))
            carry = carry + incl[:, 0:1]
        scores.append(s)
        products.append(functools.partial(lambda p, ref: _dot_nt(p, ref[0, 0].astype(BF16)), ref=v_refs[j]))
    if fox:
        carry_s[...] = carry
    update(scores, products)

    @pl.when(step == pl.num_programs(1) - 1)
    def _finish():
        o = acc_s[...] / l_s[...]
        if fox:
            o_ref[0] = _unstack_select(o, n_tok, HEAD_DIM, range(N_HEADS))
        else:
            o1 = _unstack_select(o, n_tok, HEAD_DIM, range(N_HEADS))
            o2 = _unstack_select(o, n_tok, HEAD_DIM, range(N_HEADS, 2 * N_HEADS))
            od = o1 - _diff_lambda(lam_ref, lam_init) * o2
            ms = _dot_split(od * od, _block_ones(GROUP_WIDTH, HEAD_DIM), 2) * (1.0 / HEAD_DIM)
            o_ref[0] = od * lax.rsqrt(ms + EPS) * norm_ref[...] * (1.0 - lam_init)


def _decode_attention(page_table, q, k_new, v_new, k_pool, v_pool, layer, *, fox, extras, suffix=None,
                      lam_init=0.0):
    b, n_tok, w = q.shape
    n_pages = page_table.shape[1]
    pps = PAGES_PER_STEP
    assert n_pages % pps == 0 and n_tok % SUBLANES == 0
    steps = n_pages // pps
    tok_spec = pl.BlockSpec((1, n_tok, w), lambda i, s, pt: (i, 0, 0))
    const2 = lambda i, s, pt: (0, 0)

    def page_spec(j):
        return pl.BlockSpec((1, 1, w, PAGE_SIZE),
                            lambda i, s, pt: (layer, pt[i, n_pages - 1 - (s * pps + j)], 0, 0))

    def suffix_spec(j):
        return pl.BlockSpec((1, N_HEADS, PAGE_SIZE),
                            lambda i, s, pt: (pt[i, n_pages - 1 - (s * pps + j)], 0, 0))

    rows = (N_HEADS if fox else 2 * N_HEADS) * n_tok
    scratch = [pltpu.VMEM((rows, w), BF16), pltpu.VMEM((rows, 1), F32), pltpu.VMEM((rows, 1), F32),
               pltpu.VMEM((rows, w), F32)]
    in_specs = [tok_spec] * 3
    if fox:
        sm, prow = extras
        in_specs += [pl.BlockSpec((1, n_tok, SMALL_W), lambda i, s, pt: (i, 0, 0)),
                     pl.BlockSpec(prow.shape, const2)]
        in_specs += [page_spec(j) for j in range(pps)] * 2 + [suffix_spec(j) for j in range(pps)]
        operands = (q, k_new, v_new, sm, prow) + (k_pool,) * pps + (v_pool,) * pps + (suffix,) * pps
        out_shape = [jax.ShapeDtypeStruct((b, n_tok, w), F32), jax.ShapeDtypeStruct((b, n_tok, N_HEADS), F32)]
        out_specs = [tok_spec, pl.BlockSpec((1, n_tok, N_HEADS), lambda i, s, pt: (i, 0, 0))]
        scratch += [pltpu.VMEM((rows, 1), F32), pltpu.VMEM((rows, 1), F32)]
    else:
        lam, norm = extras
        in_specs += [pl.BlockSpec(lam.shape, const2), pl.BlockSpec(norm.shape, const2)]
        in_specs += [page_spec(j) for j in range(pps)] * 2
        operands = (q, k_new, v_new, lam, norm) + (k_pool,) * pps + (v_pool,) * pps
        out_shape = jax.ShapeDtypeStruct((b, n_tok, w), F32)
        out_specs = tok_spec
    return pl.pallas_call(
        functools.partial(_decode_body, fox=fox, n_tok=n_tok, lam_init=lam_init),
        grid_spec=pltpu.PrefetchScalarGridSpec(
            num_scalar_prefetch=1, grid=(b, steps), in_specs=in_specs, out_specs=out_specs,
            scratch_shapes=scratch),
        out_shape=out_shape,
        compiler_params=_cparams(2),
        name="decode_fox" if fox else "decode_diff",
    )(page_table, *operands)


def _split_w_in(w):
    a0, b0, c0, d0 = 0, A_COLS, A_COLS + B_COLS, A_COLS + B_COLS + C_COLS
    main = [w[:, a0:a0 + 768], w[:, b0:b0 + 768], w[:, c0:c0 + C_CONV_CH],
            w[:, c0 + C_CONV_CH:c0 + C_CONV_CH + 256], w[:, d0:d0 + 256], w[:, d0 + 256:d0 + 256 + D_CONV_CH]]
    small = jnp.concatenate([w[:, a0 + 768:a0 + 772], w[:, c0 + 1024:c0 + 1028], w[:, c0 + 1028:c0 + 1032],
                             w[:, d0 + 768:d0 + 772]], axis=1)
    small_pad = jnp.pad(small, ((0, 0), (0, SMALL_W - small.shape[1])))
    return jnp.concatenate(main + [small_pad], axis=1).astype(BF16)


def _pages_transposed(cache):
    nd = cache.ndim
    moved = cache.transpose((0, 1) + tuple(range(3, nd)) + (2,))
    return moved.reshape(cache.shape[0], cache.shape[1], -1, PAGE_SIZE)


def _gdn_state_to_block(s):
    b = s.shape[0]
    eye = jnp.eye(N_HEADS, dtype=s.dtype)
    return jnp.einsum('bhkv,hg->bhkgv', s, eye).reshape(b, GROUP_WIDTH, GROUP_WIDTH)


def _gdn_block_to_state(sb):
    b = sb.shape[0]
    s5 = sb.reshape(b, N_HEADS, HEAD_DIM, N_HEADS, HEAD_DIM)
    return jnp.stack([s5[:, h, :, h, :] for h in range(N_HEADS)], axis=1)


def _ssd_state_to_block(s):
    b = s.shape[0]
    st = s.transpose(0, 3, 1, 2)
    grp = (jnp.arange(2)[:, None] == (jnp.arange(N_HEADS) // 2)[None, :]).astype(s.dtype)
    return jnp.einsum('bnhp,gh->bgnhp', st, grp).reshape(b, 2 * D_STATE, GROUP_WIDTH)


def _ssd_block_to_state(wb):
    b = wb.shape[0]
    w5 = wb.reshape(b, 2, D_STATE, N_HEADS, HEAD_DIM)
    s = jnp.stack([w5[:, h // 2, :, h, :] for h in range(N_HEADS)], axis=1)
    return s.transpose(0, 1, 3, 2)


def _row_tile(n):
    for tm in (768, 512, 384, 256, 128):
        if n % tm == 0:
            return tm
    return n


def _trunk(x, past, weights):
    (g_mix_pre, w_in, fox_forget_bias, diff_lambda, diff_norm, gdn_conv_w, gdn_A_log, gdn_dt_bias, gdn_norm,
     ssd_conv_w, ssd_conv_b, ssd_A_log, ssd_dt_bias, ssd_D, ssd_norm, w_out, g_mix_post, g_mlp_pre,
     w_mlp_up, w_mlp_down, g_mlp_post) = weights
    bsz, seq, d = x.shape
    n = bsz * seq
    tm = _row_tile(n)
    x2 = x.reshape(n, d)
    if past is not None:
        (cache_fox_k, cache_fox_v, cache_fox_logf, cache_diff_k, cache_diff_v,
         state_gdn_conv, state_gdn, state_ssd_conv, state_ssd, page_table) = past
        fox_k_pages, fox_v_pages = _pages_transposed(cache_fox_k), _pages_transposed(cache_fox_v)
        diff_k_pages, diff_v_pages = _pages_transposed(cache_diff_k), _pages_transposed(cache_diff_v)
        logf_rows = cache_fox_logf.transpose(0, 1, 3, 2).reshape(DEPTH, -1, PAGE_SIZE)
    layer_rows = []
    for l in range(DEPTH):
        lam_init = 0.8 - 0.6 * math.exp(-0.3 * l)
        outs = _in_proj(x2, g_mix_pre[l], _split_w_in(w_in[l]), tm)
        fq, fk, fv, dq, dk, dv, cqkv, cz, sz, sxbc, sm = [o.reshape(bsz, seq, -1) for o in outs]

        bias_lanes = jnp.concatenate([fox_forget_bias[l], gdn_dt_bias[l], jnp.zeros((4,), F32), ssd_dt_bias[l]])
        prow = jnp.zeros((SUBLANES, SMALL_W), F32).at[0, 0:bias_lanes.shape[0]].set(bias_lanes)
        lam = diff_lambda[l]
        dnorm = jnp.tile(diff_norm[l], N_HEADS).reshape(1, GROUP_WIDTH)
        gdn_par = jnp.zeros((SUBLANES, GROUP_WIDTH), F32)
        gdn_par = gdn_par.at[0].set(jnp.repeat(gdn_dt_bias[l], HEAD_DIM))
        gdn_par = gdn_par.at[1].set(jnp.repeat(gdn_A_log[l], HEAD_DIM))
        gdn_par = gdn_par.at[2].set(jnp.tile(gdn_norm[l], N_HEADS))
        ssd_par = jnp.zeros((SUBLANES, GROUP_WIDTH), F32)
        ssd_par = ssd_par.at[0].set(jnp.repeat(ssd_dt_bias[l], HEAD_DIM))
        ssd_par = ssd_par.at[1].set(jnp.repeat(ssd_A_log[l], HEAD_DIM))
        ssd_par = ssd_par.at[2].set(jnp.repeat(ssd_D[l], HEAD_DIM))
        ssd_par = ssd_par.at[3].set(ssd_norm[l])
        gdn_w8 = jnp.pad(gdn_conv_w[l], ((0, SUBLANES - CONV_W), (0, 0)))
        ssd_w8 = jnp.pad(ssd_conv_w[l], ((0, SUBLANES - CONV_W), (0, 0)))
        ssd_cb = ssd_conv_b[l].reshape(1, D_CONV_CH)
        state_pad = ((0, 0), (SUBLANES - (CONV_W - 1), 0), (0, 0))

        if past is None:
            out_a, logf = _prompt_attention(fq, fk, fv, fox=True, extras=(sm, prow))
            out_b = _prompt_attention(dq, dk, dv, fox=False, extras=(lam, dnorm), lam_init=lam_init)
            gdn_buf = jnp.zeros((bsz, SUBLANES, C_CONV_CH), F32)
            gdn_s0 = jnp.zeros((bsz, GROUP_WIDTH, GROUP_WIDTH), F32)
            ssd_buf = jnp.zeros((bsz, SUBLANES, D_CONV_CH), F32)
            ssd_s0 = jnp.zeros((bsz, 2 * D_STATE, GROUP_WIDTH), F32)
        else:
            suffix = _page_suffix_sums(logf_rows, l)
            out_a, logf = _decode_attention(page_table, fq, fk, fv, fox_k_pages, fox_v_pages, l, fox=True,
                                            extras=(sm, prow), suffix=suffix)
            out_b = _decode_attention(page_table, dq, dk, dv, diff_k_pages, diff_v_pages, l, fox=False,
                                      extras=(lam, dnorm), lam_init=lam_init)
            gdn_buf = jnp.pad(state_gdn_conv[l], state_pad)
            gdn_s0 = _gdn_state_to_block(state_gdn[l])
            ssd_buf = jnp.pad(state_ssd_conv[l], state_pad)
            ssd_s0 = _ssd_state_to_block(state_ssd[l])

        out_c, gdn_conv8, gdn_sb = _gdn(cqkv, cz, sm, gdn_buf, gdn_s0, gdn_w8, gdn_par)
        out_d, ssd_conv8, ssd_sb = _ssd(sxbc, sz, sm, ssd_buf, ssd_s0, ssd_w8, ssd_cb, ssd_par)

        mixer_outs = [o.reshape(n, GROUP_WIDTH) for o in (out_a, out_b, out_c, out_d)]
        x2 = _post(x2, mixer_outs, w_out[l].astype(BF16), g_mix_post[l], g_mlp_pre[l],
                   w_mlp_up[l].astype(BF16), w_mlp_down[l].astype(BF16), g_mlp_post[l], tm)

        tail = SUBLANES - (CONV_W - 1)
        layer_rows.append((
            fk.reshape(bsz, seq, N_HEADS, HEAD_DIM), fv.reshape(bsz, seq, N_HEADS, HEAD_DIM), logf,
            dk.reshape(bsz, seq, N_HEADS, 2, DIFF_QK), dv.reshape(bsz, seq, N_HEADS, HEAD_DIM),
            gdn_conv8[:, tail:], _gdn_block_to_state(gdn_sb),
            ssd_conv8[:, tail:], _ssd_block_to_state(ssd_sb)))
    stacked = tuple(jnp.stack([rows[i] for rows in layer_rows]) for i in range(len(layer_rows[0])))
    return x2.reshape(bsz, seq, d), stacked


def kernel(x_prompt, x_sample, cache_fox_k, cache_fox_v, cache_fox_logf, cache_diff_k, cache_diff_v, state_gdn_conv, state_gdn, state_ssd_conv, state_ssd, page_table, meta_tokens, g_mix_pre, w_in, fox_forget_bias, diff_lambda, diff_norm, gdn_conv_w, gdn_A_log, gdn_dt_bias, gdn_norm, ssd_conv_w, ssd_conv_b, ssd_A_log, ssd_dt_bias, ssd_D, ssd_norm, w_out, g_mix_post, g_mlp_pre, w_mlp_up, w_mlp_down, g_mlp_post):
    weights = (g_mix_pre, w_in, fox_forget_bias, diff_lambda, diff_norm, gdn_conv_w, gdn_A_log, gdn_dt_bias,
               gdn_norm, ssd_conv_w, ssd_conv_b, ssd_A_log, ssd_dt_bias, ssd_D, ssd_norm, w_out, g_mix_post,
               g_mlp_pre, w_mlp_up, w_mlp_down, g_mlp_post)
    bsz = x_prompt.shape[0]
    meta = jnp.broadcast_to(meta_tokens.astype(x_prompt.dtype)[None], (bsz,) + meta_tokens.shape)
    xp = jnp.concatenate([meta, x_prompt], axis=1)
    y_p, rows_p = _trunk(xp, None, weights)
    past = (cache_fox_k, cache_fox_v, cache_fox_logf, cache_diff_k, cache_diff_v,
            state_gdn_conv, state_gdn, state_ssd_conv, state_ssd, page_table)
    y_s, rows_s = _trunk(x_sample, past, weights)
    return (y_p[:, meta_tokens.shape[0]:], y_s) + rows_p + rows_s
```

```python
import functools
import math

import jax
import jax.numpy as jnp
from jax import lax
from jax.experimental import pallas as pl
from jax.experimental.pallas import tpu as pltpu

F32 = jnp.float32
BF16 = jnp.bfloat16

D_MODEL = 1024
DEPTH = 2
GROUP_WIDTH = 256
N_HEADS = 4
HEAD_DIM = 64
DIFF_QK = 32
D_STATE = 64
CONV_W = 4
C_CONV_CH = 3 * GROUP_WIDTH
D_CONV_CH = GROUP_WIDTH + 2 * 2 * D_STATE
A_COLS = 3 * GROUP_WIDTH + N_HEADS
B_COLS = 3 * GROUP_WIDTH
C_COLS = C_CONV_CH + GROUP_WIDTH + 2 * N_HEADS
D_FF = 4 * D_MODEL
PAGE_SIZE = 128
EPS = 1e-6
NEG = -1e30

LANES = 128
SUBLANES = 8
SMALL_W = 128
ATTN_Q_TILE = 256
ATTN_K_BLOCK = 256
GDN_CHUNK = 64
GDN_PRE_TILE = 256
SSD_CHUNK = 128
PAGES_PER_STEP = 16
SUFFIX_ROWS = 256
VMEM_LIMIT = 56 * 1024 * 1024

PROJ_WIDTHS = (256, 256, 256, 256, 256, 256, C_CONV_CH, 256, 256, D_CONV_CH, SMALL_W)


def _softplus(x):
    return jnp.maximum(x, 0.0) + jnp.log1p(jnp.exp(-jnp.abs(x)))


def _sigmoid(x):
    return 1.0 / (1.0 + jnp.exp(-x))


def _silu(x):
    return x * _sigmoid(x)


def _dot(a, b, prec=None):
    return jnp.dot(a, b, preferred_element_type=F32, precision=prec)


def _dot_nt(a, b, prec=None):
    return lax.dot_general(a, b, (((1,), (1,)), ((), ())), preferred_element_type=F32, precision=prec)


def _dot_tn(a, b, prec=None):
    return lax.dot_general(a, b, (((0,), (0,)), ((), ())), preferred_element_type=F32, precision=prec)


def _bf16_terms(x, terms):
    pieces = []
    r = x
    for t in range(terms):
        p = r.astype(BF16)
        pieces.append(p)
        if t + 1 < terms:
            r = r - p.astype(F32)
    return pieces


def _dot_split(x, w, terms, dot=_dot):
    w_bf = w.astype(BF16)
    acc = None
    for p in _bf16_terms(x, terms):
        part = dot(p, w_bf)
        acc = part if acc is None else acc + part
    return acc


def _dot_split_rhs(w, x, terms, dot=_dot):
    w_bf = w.astype(BF16)
    acc = None
    for p in _bf16_terms(x, terms):
        part = dot(w_bf, p)
        acc = part if acc is None else acc + part
    return acc


def _mm1(a, b):
    return _dot(a.astype(BF16), b.astype(BF16))


def _iota(shape, dim):
    return lax.broadcasted_iota(jnp.int32, shape, dim)


def _lane_group_mask(width, group, idx):
    return (_iota((1, width), 1) // group) == idx


def _block_ones(n, group):
    return ((_iota((n, n), 0) // group) == (_iota((n, n), 1) // group)).astype(F32)


def _lower_tri(n, strict=False):
    ri, ci = _iota((n, n), 0), _iota((n, n), 1)
    return (ri > ci) if strict else (ri >= ci)


def _stack_masked(x, group, groups):
    zero = jnp.zeros_like(x)
    return jnp.concatenate([jnp.where(_lane_group_mask(x.shape[1], group, g), x, zero) for g in groups], axis=0)


def _unstack_sum(xs, rows, n_groups):
    out = xs[0:rows]
    for g in range(1, n_groups):
        out = out + xs[g * rows:(g + 1) * rows]
    return out


def _unstack_select(xs, rows, group, blocks):
    out = None
    for g, blk in enumerate(blocks):
        part = jnp.where(_lane_group_mask(xs.shape[1], group, g), xs[blk * rows:(blk + 1) * rows], 0.0)
        out = part if out is None else out + part
    return out


def _pad_rows(x, rows):
    if x.shape[0] == rows:
        return x
    return jnp.concatenate([x, jnp.zeros((rows - x.shape[0], x.shape[1]), x.dtype)], axis=0)


def _cparams(n_grid_dims):
    return pltpu.CompilerParams(dimension_semantics=("arbitrary",) * n_grid_dims,
                                vmem_limit_bytes=VMEM_LIMIT)


def _in_proj_body(x_ref, g_ref, w_ref, *out_refs):
    x = x_ref[...]
    ms = jnp.mean(x * x, axis=-1, keepdims=True)
    h = (x * lax.rsqrt(ms + EPS) * g_ref[...]).astype(BF16)
    off = 0
    for o_ref, width in zip(out_refs, PROJ_WIDTHS):
        o_ref[...] = _dot(h, w_ref[:, off:off + width])
        off += width


def _in_proj(x2d, g, w_main, tm):
    n, d = x2d.shape
    assert n % tm == 0
    return pl.pallas_call(
        _in_proj_body,
        grid=(n // tm,),
        in_specs=[pl.BlockSpec((tm, d), lambda i: (i, 0)),
                  pl.BlockSpec((1, d), lambda i: (0, 0)),
                  pl.BlockSpec(w_main.shape, lambda i: (0, 0))],
        out_specs=[pl.BlockSpec((tm, w), lambda i: (i, 0)) for w in PROJ_WIDTHS],
        out_shape=[jax.ShapeDtypeStruct((n, w), F32) for w in PROJ_WIDTHS],
        compiler_params=_cparams(1),
        name="in_proj",
    )(x2d, g.reshape(1, d), w_main)


def _post_body(x_ref, oa_ref, ob_ref, oc_ref, od_ref, wo_ref, gpost_ref, gpre_ref, wu_ref, wd_ref,
               gmlp_ref, y_ref, *, ff_chunk):
    mix = None
    for i, o_ref in enumerate((oa_ref, ob_ref, oc_ref, od_ref)):
        part = _dot(o_ref[...].astype(BF16), wo_ref[i * GROUP_WIDTH:(i + 1) * GROUP_WIDTH, :])
        mix = part if mix is None else mix + part
    ms = jnp.mean(mix * mix, axis=-1, keepdims=True)
    x1 = x_ref[...] + mix * lax.rsqrt(ms + EPS) * gpost_ref[...]
    ms = jnp.mean(x1 * x1, axis=-1, keepdims=True)
    h = (x1 * lax.rsqrt(ms + EPS) * gpre_ref[...]).astype(BF16)
    acc = None
    for c in range(D_FF // ff_chunk):
        u = jnp.maximum(_dot(h, wu_ref[:, c * ff_chunk:(c + 1) * ff_chunk]), 0.0)
        part = _dot((u * u).astype(BF16), wd_ref[c * ff_chunk:(c + 1) * ff_chunk, :])
        acc = part if acc is None else acc + part
    ms = jnp.mean(acc * acc, axis=-1, keepdims=True)
    y_ref[...] = x1 + acc * lax.rsqrt(ms + EPS) * gmlp_ref[...]


def _post(x2d, outs, w_out, g_post, g_pre, w_up, w_down, g_mlp, tm):
    n, d = x2d.shape
    assert n % tm == 0
    row = lambda i: (i, 0)
    const = lambda i: (0, 0)
    single = pl.Buffered(1)
    return pl.pallas_call(
        functools.partial(_post_body, ff_chunk=512),
        grid=(n // tm,),
        in_specs=[pl.BlockSpec((tm, d), row)]
        + [pl.BlockSpec((tm, GROUP_WIDTH), row)] * 4
        + [pl.BlockSpec(w_out.shape, const, pipeline_mode=single),
           pl.BlockSpec((1, d), const), pl.BlockSpec((1, d), const),
           pl.BlockSpec(w_up.shape, const, pipeline_mode=single),
           pl.BlockSpec(w_down.shape, const, pipeline_mode=single),
           pl.BlockSpec((1, d), const)],
        out_specs=pl.BlockSpec((tm, d), row),
        out_shape=jax.ShapeDtypeStruct((n, d), F32),
        compiler_params=_cparams(1),
        name="post_mlp",
    )(x2d, *outs, w_out, g_post.reshape(1, d), g_pre.reshape(1, d), w_up, w_down, g_mlp.reshape(1, d))


def _diff_lambda(lam_ref, lam_init):
    lp = lam_ref[...]
    return (jnp.exp(jnp.sum(lp[0:1] * lp[1:2], axis=-1, keepdims=True))
            - jnp.exp(jnp.sum(lp[2:3] * lp[3:4], axis=-1, keepdims=True)) + lam_init)


def _stack_rows(x, size):
    return jnp.concatenate([x[:, h:h + 1] for h in range(N_HEADS)], axis=0)


def _stack_lanes(x, size):
    return jnp.concatenate([jnp.broadcast_to(x[h:h + 1, :], (size, x.shape[1])) for h in range(N_HEADS)],
                           axis=0)


def _prompt_attn_body(*refs, fox, seq, lam_init):
    if fox:
        (q_ref, k_ref, v_ref, sm_ref, prow_ref, o_ref, kt_ref, vt_ref, logf_ref,
         kb_s, qt_s, vt_s, m_s, l_s, acc_s, ckx_s) = refs
        groups = list(range(N_HEADS))
        group = HEAD_DIM
    else:
        (q_ref, k_ref, v_ref, lam_ref, norm_ref, o_ref, kt_ref, vt_ref,
         kb_s, qt_s, vt_s, m_s, l_s, acc_s) = refs
        groups = [2 * h for h in range(N_HEADS)] + [2 * h + 1 for h in range(N_HEADS)]
        group = DIFF_QK
    tq, tk = ATTN_Q_TILE, ATTN_K_BLOCK
    n_q = -(-seq // tq)
    seq_pad = kb_s.shape[0]
    last_valid = seq - (n_q - 1) * tq
    n_blocks = len(groups)
    width = n_blocks * tq

    kb_s[0:seq, :] = k_ref[0].astype(BF16)
    kb_s[seq:seq_pad, :] = jnp.zeros((seq_pad - seq, GROUP_WIDTH), BF16)
    if seq_pad > n_q * tq:
        zeros = jnp.zeros((GROUP_WIDTH, seq_pad - n_q * tq), BF16)
        qt_s[:, n_q * tq:seq_pad] = zeros
        vt_s[:, n_q * tq:seq_pad] = zeros
    if fox:
        if seq_pad > n_q * tq:
            ckx_s[n_q * tq:seq_pad, :] = jnp.zeros((seq_pad - n_q * tq, N_HEADS * LANES), F32)
        bias_row = prow_ref[0:1, :]
        tri_lo = _lower_tri(tq).astype(F32)

    def stage_tile(start, valid, carry):
        qt_s[:, pl.ds(start, tq)] = (_pad_rows(q_ref[0, pl.ds(start, valid), :], tq)
                                     * (group ** -0.5)).T.astype(BF16)
        v_t = _pad_rows(v_ref[0, pl.ds(start, valid), :], tq).T
        vt_s[:, pl.ds(start, tq)] = v_t.astype(BF16)
        vt_ref[0, :, pl.ds(start, valid)] = v_t[:, 0:valid]
        kt_ref[0, :, pl.ds(start, valid)] = _pad_rows(k_ref[0, pl.ds(start, valid), :], tq).T[:, 0:valid]
        if not fox:
            return carry
        lf = -_softplus(-(_pad_rows(sm_ref[0, pl.ds(start, valid), :], tq) + bias_row))
        logf_ref[0, pl.ds(start, valid), :] = lf[0:valid, 0:N_HEADS]
        cc = _dot_split_rhs(tri_lo, lf, 3) + carry
        ckx_s[pl.ds(start, tq), :] = jnp.concatenate(
            [jnp.broadcast_to(cc[:, h:h + 1], (tq, LANES)) for h in range(N_HEADS)], axis=1)
        return cc[tq - 1:tq, :]

    carry = jnp.zeros((1, SMALL_W), F32)
    carry = lax.fori_loop(0, n_q - 1, lambda i, c: stage_tile(pl.multiple_of(i * tq, tq), tq, c), carry)
    stage_tile((n_q - 1) * tq, last_valid, carry)

    row_group = _iota((GROUP_WIDTH, 1), 0) // group
    lane_q = _iota((1, width), 1) % tq
    key_off = _iota((tk, 1), 0)

    def q_tile(idx, valid):
        static = isinstance(idx, int)
        qs0 = idx * tq if static else pl.multiple_of(idx * tq, tq)
        qt = qt_s[:, pl.ds(qs0, tq)]
        zero = jnp.zeros_like(qt)
        qstack = jnp.concatenate([jnp.where(row_group == g, qt, zero) for g in groups], axis=1)
        m_s[...] = jnp.full(m_s.shape, NEG, F32)
        l_s[...] = jnp.zeros(l_s.shape, F32)
        acc_s[...] = jnp.zeros(acc_s.shape, F32)

        def scores(kb0, masked):
            s = _dot(kb_s[pl.ds(kb0, tk), :], qstack)
            if fox:
                ck = ckx_s[pl.ds(kb0, tk), :]
                s = s - jnp.concatenate([ck[:, h * LANES:(h + 1) * LANES] for h in range(N_HEADS)
                                         for _ in range(tq // LANES)], axis=1)
            if masked:
                s = jnp.where((kb0 + key_off) <= (qs0 + lane_q), s, NEG)
            return s

        def absorb(s, kb0):
            m_old = m_s[...]
            m_new = jnp.maximum(m_old, jnp.max(s, axis=0, keepdims=True))
            alpha = jnp.exp(m_old - m_new)
            p = jnp.exp(s - m_new)
            l_s[...] = alpha * l_s[...] + jnp.sum(p, axis=0, keepdims=True)
            m_s[...] = m_new
            pb = p.astype(BF16)
            for b in range(n_blocks):
                h = b % N_HEADS
                upd = _dot(vt_s[h * HEAD_DIM:(h + 1) * HEAD_DIM, pl.ds(kb0, tk)], pb[:, b * tq:(b + 1) * tq])
                rows = slice(b * HEAD_DIM, (b + 1) * HEAD_DIM)
                acc_s[rows, :] = acc_s[rows, :] * alpha[:, b * tq:(b + 1) * tq] + upd

        def pair(kb_a, kb_b, mask_b):
            sa = scores(kb_a, False)
            sb = scores(kb_b, mask_b)
            absorb(sa, kb_a)
            absorb(sb, kb_b)

        def single(kb0):
            absorb(scores(kb0, True), kb0)

        n_before = (idx * tq) // tk

        def pair_step(j, c):
            kb = pl.multiple_of(2 * j * tk, 2 * tk)
            pair(kb, pl.multiple_of(kb + tk, tk), False)
            return c

        lax.fori_loop(0, n_before // 2, pair_step, 0)
        if static:
            last = n_before * tk
            if n_before % 2:
                pair(last - tk, last, True)
            else:
                single(last)
        else:
            last = pl.multiple_of(n_before * tk, tk)
            pl.when(n_before % 2 == 1)(lambda: pair(pl.multiple_of(last - tk, tk), last, True))
            pl.when(n_before % 2 == 0)(lambda: single(last))

        l = l_s[...]
        o_t = jnp.concatenate([acc_s[b * HEAD_DIM:(b + 1) * HEAD_DIM, :] / l[:, b * tq:(b + 1) * tq]
                               for b in range(n_blocks)], axis=0)
        if fox:
            o = o_t.T
        else:
            od = o_t[0:GROUP_WIDTH] - _diff_lambda(lam_ref, lam_init) * o_t[GROUP_WIDTH:]
            normed = []
            for h in range(N_HEADS):
                blk = od[h * HEAD_DIM:(h + 1) * HEAD_DIM]
                ms = jnp.mean(blk * blk, axis=0, keepdims=True)
                normed.append(blk * lax.rsqrt(ms + EPS))
            o = jnp.concatenate(normed, axis=0).T * norm_ref[...] * (1.0 - lam_init)
        o_ref[0, pl.ds(qs0, valid), :] = o[0:valid]

    def loop_body(i, c):
        q_tile(i, tq)
        return c

    lax.fori_loop(0, n_q - 1, loop_body, 0)
    q_tile(n_q - 1, last_valid)


def _prompt_attention(q, k, v, *, fox, extras, lam_init=0.0):
    b, seq, w = q.shape
    tq, tk = ATTN_Q_TILE, ATTN_K_BLOCK
    n_q = -(-seq // tq)
    seq_pad = -(-(n_q * tq) // tk) * tk
    n_blocks = N_HEADS if fox else 2 * N_HEADS
    seq_spec = pl.BlockSpec((1, seq, w), lambda i: (i, 0, 0))
    const2 = lambda i: (0, 0)
    scratch = [pltpu.VMEM((seq_pad, w), BF16), pltpu.VMEM((w, seq_pad), BF16), pltpu.VMEM((w, seq_pad), BF16),
               pltpu.VMEM((1, n_blocks * tq), F32), pltpu.VMEM((1, n_blocks * tq), F32),
               pltpu.VMEM((n_blocks * HEAD_DIM, tq), F32)]
    t_spec = pl.BlockSpec((1, w, seq), lambda i: (i, 0, 0))
    t_shape = jax.ShapeDtypeStruct((b, w, seq), F32)
    if fox:
        sm, prow = extras
        in_specs = [seq_spec] * 3 + [pl.BlockSpec((1, seq, SMALL_W), lambda i: (i, 0, 0)),
                                     pl.BlockSpec(prow.shape, const2)]
        out_shape = [jax.ShapeDtypeStruct((b, seq, w), F32)] + [t_shape] * 2 + [
            jax.ShapeDtypeStruct((b, seq, N_HEADS), F32)]
        out_specs = [seq_spec, t_spec, t_spec, pl.BlockSpec((1, seq, N_HEADS), lambda i: (i, 0, 0))]
        scratch = scratch + [pltpu.VMEM((seq_pad, N_HEADS * LANES), F32)]
    else:
        lam, norm = extras
        in_specs = [seq_spec] * 3 + [pl.BlockSpec(lam.shape, const2), pl.BlockSpec(norm.shape, const2)]
        out_shape = [jax.ShapeDtypeStruct((b, seq, w), F32)] + [t_shape] * 2
        out_specs = [seq_spec, t_spec, t_spec]
    return pl.pallas_call(
        functools.partial(_prompt_attn_body, fox=fox, seq=seq, lam_init=lam_init),
        grid=(b,),
        in_specs=in_specs,
        out_specs=out_specs,
        out_shape=out_shape,
        scratch_shapes=scratch,
        compiler_params=_cparams(1),
        name="prompt_fox" if fox else "prompt_diff",
    )(q, k, v, *extras)


def _fill_conv_scratch(x_ref, buf_ref, xpad_s, new_ref, seq):
    xpad_s[0:SUBLANES, :] = buf_ref[0]
    xpad_s[SUBLANES:SUBLANES + seq, :] = x_ref[0]
    tail = xpad_s.shape[0] - SUBLANES - seq
    if tail:
        xpad_s[SUBLANES + seq:, :] = jnp.zeros((tail, xpad_s.shape[1]), F32)
    new_ref[0] = xpad_s[seq:seq + SUBLANES, :]


def _conv_chunk(xpad_s, w_ref, start, size):
    n = size + SUBLANES
    win = xpad_s[pl.ds(start, n), :]
    out = None
    for i in range(CONV_W):
        first = SUBLANES - (CONV_W - 1) + i
        part = pltpu.roll(win, n - first, axis=0)[0:size] * w_ref[i:i + 1, :]
        out = part if out is None else out + part
    return out


def _head_selector(first_lanes):
    parts = [(_iota((SMALL_W, GROUP_WIDTH), 0) == first + _iota((SMALL_W, GROUP_WIDTH), 1) // HEAD_DIM)
             for first in first_lanes]
    return jnp.concatenate(parts, axis=1).astype(F32)


def _expand_heads(sm, first_lane):
    return _dot_split(sm, _head_selector([first_lane]), 3)


def _head_rows(x_exp):
    sel = (_iota((SUBLANES, GROUP_WIDTH), 0) == _iota((SUBLANES, GROUP_WIDTH), 1) // HEAD_DIM)
    return _dot_split_rhs(sel.astype(F32) * (1.0 / HEAD_DIM), x_exp, 3, _dot_nt)


def _inv_unit_lower(mats, n, blk):
    ri = _iota((n, n), 0)
    ci = _iota((n, n), 1)
    eye = (ri == ci).astype(F32)
    base = min(16, blk)
    in_base = (ri // base) == (ci // base)
    ps = [-jnp.where(in_base, a, 0.0) for a in mats]
    xs = [eye + p for p in ps]
    s = 1
    while 2 * s < base:
        ps = [_mm1(p, p) for p in ps]
        xs = [x + _mm1(x, p) for x, p in zip(xs, ps)]
        s *= 2
    size = base
    while size < blk:
        off_diag = ((ri // (2 * size)) == (ci // (2 * size))) & ((ri // size) != (ci // size))
        ts = [_mm1(x, jnp.where(off_diag, a, 0.0)) for x, a in zip(xs, mats)]
        xs = [x - _mm1(t, x) for x, t in zip(xs, ts)]
        size *= 2
    return xs


def _tile_schedule(seq, tile, fn):
    n_tiles = -(-seq // tile)
    if n_tiles > 1:
        def body(i, c):
            fn(pl.multiple_of(i * tile, tile), tile)
            return c
        lax.fori_loop(0, n_tiles - 1, body, 0)
    fn((n_tiles - 1) * tile, seq - (n_tiles - 1) * tile)


def _gdn_body(x_ref, z_ref, sm_ref, buf_ref, s0_ref, w_ref, par_ref,
              o_ref, new_ref, sn_ref, xpad_s, u_s, w_s, qe_s, kd_s, pm_s, eg_s, o_s, st_s, *, seq, pre_tile):
    _fill_conv_scratch(x_ref, buf_ref, xpad_s, new_ref, seq)
    st_s[...] = s0_ref[0]
    size = GDN_CHUNK
    n = N_HEADS * size
    dtb = par_ref[0:1, :]
    a_neg = -jnp.exp(par_ref[1:2, :])
    gain = par_ref[2:3, :]
    head_ones = _block_ones(GROUP_WIDTH, HEAD_DIM)
    heads = range(N_HEADS)
    ri = _iota((n, n), 0)
    ci = _iota((n, n), 1)
    same = (ri // size) == (ci // size)
    tri = _lower_tri(size).astype(F32)
    mean_row = jnp.full((SUBLANES, GROUP_WIDTH), 1.0 / HEAD_DIM, F32)

    def prepare(start, valid):
        act = _silu(_conv_chunk(xpad_s, w_ref, start, pre_tile))
        q = act[:, 0:GROUP_WIDTH]
        k = act[:, GROUP_WIDTH:2 * GROUP_WIDTH]
        v = act[:, 2 * GROUP_WIDTH:3 * GROUP_WIDTH]
        q = q * lax.rsqrt(_dot_split(q * q, head_ones, 2) + EPS) * (HEAD_DIM ** -0.5)
        k = k * lax.rsqrt(_dot_split(k * k, head_ones, 2) + EPS)
        ab = _dot_split(_pad_rows(sm_ref[0, pl.ds(start, valid), :], pre_tile), _head_selector([4, 8]), 3)
        g = a_neg * _softplus(ab[:, 0:GROUP_WIDTH] + dtb)
        beta = _sigmoid(ab[:, GROUP_WIDTH:])
        if valid < pre_tile:
            live = _iota((pre_tile, 1), 0) < valid
            g = jnp.where(live, g, 0.0)
            beta = jnp.where(live, beta, 0.0)
        first_chunk = start // size if isinstance(start, int) else lax.div(start, size)
        chunks = range(pre_tile // size)
        part = lambda x: [x[c * size:(c + 1) * size] for c in chunks]
        qc, kc, vc, bc = part(q), part(k), part(v), part(beta)
        gam = [_dot_split_rhs(tri, gi, 3) for gi in part(g)]
        egam = [jnp.exp(x) for x in gam]
        gcol = [jnp.concatenate([x[:, h * HEAD_DIM:h * HEAD_DIM + 1] for h in heads], axis=0) for x in gam]
        grow = [_dot_split_rhs(mean_row, _stack_masked(x, HEAD_DIM, heads), 3, _dot_nt)[0:1] for x in gam]
        bcol = [jnp.concatenate([x[:, h * HEAD_DIM:h * HEAD_DIM + 1] for h in heads], axis=0) for x in bc]
        decay = [jnp.exp(jnp.where(same & (ri >= ci), a - b, -jnp.inf)) for a, b in zip(gcol, grow)]
        ks = [_stack_masked(x, HEAD_DIM, heads).astype(BF16) for x in kc]
        qs = [_stack_masked(x, HEAD_DIM, heads).astype(BF16) for x in qc]
        kk = [_dot_nt(x, x) for x in ks]
        a_mat = [jnp.where(same & (ri > ci), x * d, 0.0) * b for x, d, b in zip(kk, decay, bcol)]
        t_inv = _inv_unit_lower(a_mat, n, size)
        rhs = [jnp.concatenate([_stack_masked(x * b, HEAD_DIM, heads),
                                _stack_masked(y * e * b, HEAD_DIM, heads)], axis=1)
               for x, y, e, b in zip(vc, kc, egam, bc)]
        uw = [_mm1(t, r) for t, r in zip(t_inv, rhs)]
        qk = [_dot_nt(x, y) for x, y in zip(qs, ks)]
        for c in chunks:
            rows = pl.ds(start + c * size, size)
            u_s[rows, :] = _unstack_sum(uw[c][:, 0:GROUP_WIDTH], size, N_HEADS)
            w_s[rows, :] = _unstack_sum(uw[c][:, GROUP_WIDTH:], size, N_HEADS).astype(BF16)
            qe_s[rows, :] = (qc[c] * egam[c]).astype(BF16)
            g_last = gam[c][size - 1:size, :]
            kd_s[rows, :] = (kc[c] * jnp.exp(g_last - gam[c])).astype(BF16)
            idx = first_chunk + c
            pm_s[idx] = (qk[c] * decay[c]).astype(BF16)
            eg_start = idx * SUBLANES if isinstance(idx, int) else pl.multiple_of(idx * SUBLANES, SUBLANES)
            eg_s[pl.ds(eg_start, SUBLANES), :] = jnp.broadcast_to(jnp.exp(g_last), (SUBLANES, GROUP_WIDTH))

    _tile_schedule(seq, pre_tile, prepare)

    def scan(i, carry):
        rows = pl.ds(pl.multiple_of(i * size, size), size)
        state = st_s[...]
        ws = _dot(jnp.concatenate([w_s[rows, :], qe_s[rows, :]], axis=0), state.astype(BF16))
        v_new = u_s[rows, :] - ws[0:size]
        intra = _dot(pm_s[i], _stack_masked(v_new, HEAD_DIM, heads).astype(BF16))
        o_s[rows, :] = _unstack_sum(intra, size, N_HEADS) + ws[size:]
        eg = eg_s[pl.ds(pl.multiple_of(i * SUBLANES, SUBLANES), 1), :]
        st_s[...] = state * eg + _dot_tn(kd_s[rows, :], v_new.astype(BF16)) * head_ones
        return carry

    n_scanned = -(-seq // size)
    lax.fori_loop(0, n_scanned, scan, 0)
    sn_ref[0] = st_s[...]
    if o_s.shape[0] > n_scanned * size:
        o_s[n_scanned * size:, :] = jnp.zeros((o_s.shape[0] - n_scanned * size, GROUP_WIDTH), F32)

    def finish(start, valid):
        o = o_s[pl.ds(start, pre_tile), :]
        ms = _dot_split(o * o, head_ones, 2) * (1.0 / HEAD_DIM)
        y = o * lax.rsqrt(ms + EPS) * gain
        o_ref[0, pl.ds(start, valid), :] = y[0:valid] * _silu(z_ref[0, pl.ds(start, valid), :])

    _tile_schedule(seq, pre_tile, finish)


def _gdn(x, z, sm, buf8, s0, conv_w8, par):
    b, seq, ch = x.shape
    pre_tile = GDN_PRE_TILE if seq > GDN_CHUNK else GDN_CHUNK
    rows_pad = -(-seq // pre_tile) * pre_tile
    n_chunks = rows_pad // GDN_CHUNK
    bspec = lambda shape: pl.BlockSpec((1,) + shape, lambda i: (i, 0, 0))
    const2 = lambda i: (0, 0)
    return pl.pallas_call(
        functools.partial(_gdn_body, seq=seq, pre_tile=pre_tile),
        grid=(b,),
        in_specs=[bspec((seq, ch)), bspec((seq, GROUP_WIDTH)), bspec((seq, SMALL_W)), bspec((SUBLANES, ch)),
                  bspec((GROUP_WIDTH, GROUP_WIDTH)), pl.BlockSpec(conv_w8.shape, const2),
                  pl.BlockSpec(par.shape, const2)],
        out_specs=[bspec((seq, GROUP_WIDTH)), bspec((SUBLANES, ch)), bspec((GROUP_WIDTH, GROUP_WIDTH))],
        out_shape=[jax.ShapeDtypeStruct((b, seq, GROUP_WIDTH), F32),
                   jax.ShapeDtypeStruct((b, SUBLANES, ch), F32),
                   jax.ShapeDtypeStruct((b, GROUP_WIDTH, GROUP_WIDTH), F32)],
        scratch_shapes=[pltpu.VMEM((rows_pad + 2 * SUBLANES, ch), F32),
                        pltpu.VMEM((rows_pad, GROUP_WIDTH), F32),
                        pltpu.VMEM((rows_pad, GROUP_WIDTH), BF16),
                        pltpu.VMEM((rows_pad, GROUP_WIDTH), BF16),
                        pltpu.VMEM((rows_pad, GROUP_WIDTH), BF16),
                        pltpu.VMEM((n_chunks, N_HEADS * GDN_CHUNK, N_HEADS * GDN_CHUNK), BF16),
                        pltpu.VMEM((n_chunks * SUBLANES, GROUP_WIDTH), F32),
                        pltpu.VMEM((rows_pad, GROUP_WIDTH), F32),
                        pltpu.VMEM((GROUP_WIDTH, GROUP_WIDTH), F32)],
        compiler_params=_cparams(1),
        name="gdn",
    )(x, z, sm, buf8, s0, conv_w8, par)


def _ssd_body(x_ref, z_ref, sm_ref, buf_ref, s0_ref, w_ref, cb_ref, par_ref,
              o_ref, new_ref, sn_ref, xpad_s, st_s, *, seq):
    _fill_conv_scratch(x_ref, buf_ref, xpad_s, new_ref, seq)
    st_s[...] = s0_ref[0]
    size = SSD_CHUNK
    dtb = par_ref[0:1, :]
    a_neg = -jnp.exp(par_ref[1:2, :])
    d_skip = par_ref[2:3, :]
    gain = par_ref[3:4, :]
    n_state = 2 * D_STATE
    live_state = ((_iota((n_state, GROUP_WIDTH), 0) // D_STATE)
                  == (_iota((n_state, GROUP_WIDTH), 1) // (2 * HEAD_DIM))).astype(F32)

    def chunk(start, valid):
        pre = _silu(_conv_chunk(xpad_s, w_ref, start, size) + cb_ref[...])
        sx = pre[:, 0:GROUP_WIDTH]
        sb = pre[:, GROUP_WIDTH:GROUP_WIDTH + n_state]
        sc = pre[:, GROUP_WIDTH + n_state:]
        sm = _pad_rows(sm_ref[0, pl.ds(start, valid), :], size)
        dt = _softplus(_expand_heads(sm, 12) + dtb)
        if valid < size:
            dt = jnp.where(_iota((size, 1), 0) < valid, dt, 0.0)
        a = dt * a_neg
        tri = _lower_tri(size)
        cum = _dot_split_rhs(tri.astype(F32), a, 3)
        ecum = jnp.exp(cum)
        cum_rows = _head_rows(cum)
        xdt = sx * dt
        sb_bf = sb.astype(BF16)
        cb = [_dot_nt(jnp.where(_lane_group_mask(n_state, D_STATE, g), sc, 0.0).astype(BF16), sb_bf)
              for g in range(2)]
        y = None
        for h in range(N_HEADS):
            dec = jnp.exp(jnp.where(tri, cum[:, h * HEAD_DIM:h * HEAD_DIM + 1] - cum_rows[h:h + 1, :], -jnp.inf))
            xh = jnp.where(_lane_group_mask(GROUP_WIDTH, HEAD_DIM, h), xdt, 0.0).astype(BF16)
            part = _dot((cb[h // 2] * dec).astype(BF16), xh)
            y = part if y is None else y + part
        state = st_s[...]
        y = y + _dot(sc.astype(BF16), state.astype(BF16)) * ecum
        c_last = cum[size - 1:size, :]
        st_s[...] = (state * jnp.exp(c_last)
                     + _dot_tn(sb_bf, (xdt * jnp.exp(c_last - cum)).astype(BF16)) * live_state)
        ys = (y + d_skip * sx)[0:valid] * _silu(z_ref[0, pl.ds(start, valid), :])
        ms = jnp.mean(ys * ys, axis=-1, keepdims=True)
        o_ref[0, pl.ds(start, valid), :] = ys * lax.rsqrt(ms + EPS) * gain

    _tile_schedule(seq, size, chunk)
    sn_ref[0] = st_s[...]


def _ssd(x, z, sm, buf8, s0, conv_w8, conv_b, par):
    b, seq, ch = x.shape
    n_state = 2 * D_STATE
    seq_pad = -(-seq // SSD_CHUNK) * SSD_CHUNK
    bspec = lambda shape: pl.BlockSpec((1,) + shape, lambda i: (i, 0, 0))
    const2 = lambda i: (0, 0)
    return pl.pallas_call(
        functools.partial(_ssd_body, seq=seq),
        grid=(b,),
        in_specs=[bspec((seq, ch)), bspec((seq, GROUP_WIDTH)), bspec((seq, SMALL_W)), bspec((SUBLANES, ch)),
                  bspec((n_state, GROUP_WIDTH)), pl.BlockSpec(conv_w8.shape, const2),
                  pl.BlockSpec(conv_b.shape, const2), pl.BlockSpec(par.shape, const2)],
        out_specs=[bspec((seq, GROUP_WIDTH)), bspec((SUBLANES, ch)), bspec((n_state, GROUP_WIDTH))],
        out_shape=[jax.ShapeDtypeStruct((b, seq, GROUP_WIDTH), F32),
                   jax.ShapeDtypeStruct((b, SUBLANES, ch), F32),
                   jax.ShapeDtypeStruct((b, n_state, GROUP_WIDTH), F32)],
        scratch_shapes=[pltpu.VMEM((seq_pad + 2 * SUBLANES, ch), F32), pltpu.VMEM((n_state, GROUP_WIDTH), F32)],
        compiler_params=_cparams(1),
        name="ssd",
    )(x, z, sm, buf8, s0, conv_w8, conv_b, par)


def _suffix_body(x_ref, o_ref):
    later = (_iota((PAGE_SIZE, PAGE_SIZE), 0) >= _iota((PAGE_SIZE, PAGE_SIZE), 1)).astype(F32)
    o_ref[0] = _dot_split(x_ref[0], later, 3)


def _page_suffix_sums(logf_rows, layer):
    n_rows = logf_rows.shape[1]
    rows = SUFFIX_ROWS if n_rows % SUFFIX_ROWS == 0 else n_rows
    out = pl.pallas_call(
        _suffix_body,
        grid=(n_rows // rows,),
        in_specs=[pl.BlockSpec((1, rows, PAGE_SIZE), lambda i: (layer, i, 0))],
        out_specs=pl.BlockSpec((1, rows, PAGE_SIZE), lambda i: (0, i, 0)),
        out_shape=jax.ShapeDtypeStruct((1, n_rows, PAGE_SIZE), F32),
        compiler_params=_cparams(1),
        name="page_suffix",
    )(logf_rows)
    return out.reshape(n_rows // N_HEADS, N_HEADS, PAGE_SIZE)


def _decode_body(pt_ref, *refs, fox, n_tok, lam_init):
    pps = PAGES_PER_STEP
    if fox:
        q_ref, kn_ref, vn_ref, sm_ref, prow_ref = refs[:5]
        rest = refs[5:]
        k_refs, v_refs, r_refs = rest[:pps], rest[pps:2 * pps], rest[2 * pps:3 * pps]
        o_ref, logf_ref, qs_s, m_s, l_s, acc_s, rq_s, carry_s = rest[3 * pps:]
        groups = list(range(N_HEADS))
        group = HEAD_DIM
    else:
        q_ref, kn_ref, vn_ref, lam_ref, norm_ref = refs[:5]
        rest = refs[5:]
        k_refs, v_refs = rest[:pps], rest[pps:2 * pps]
        o_ref, qs_s, m_s, l_s, acc_s = rest[2 * pps:]
        groups = [2 * h for h in range(N_HEADS)] + [2 * h + 1 for h in range(N_HEADS)]
        group = DIFF_QK
    step = pl.program_id(1)
    rows = len(groups) * n_tok
    t = PAGE_SIZE

    def update(scores, value_products):
        m_old = m_s[...]
        top = scores[0]
        for s in scores[1:]:
            top = jnp.maximum(top, s)
        m_new = jnp.maximum(m_old, jnp.max(top, axis=-1, keepdims=True))
        alpha = jnp.exp(m_old - m_new)
        probs = [jnp.exp(s - m_new) for s in scores]
        total = probs[0]
        for p in probs[1:]:
            total = total + p
        l_s[...] = alpha * l_s[...] + jnp.sum(total, axis=-1, keepdims=True)
        pv = None
        for p, product in zip(probs, value_products):
            part = product(p.astype(BF16))
            pv = part if pv is None else pv + part
        acc_s[...] = alpha * acc_s[...] + pv
        m_s[...] = m_new

    @pl.when(step == 0)
    def _init():
        qs_s[...] = _stack_masked((q_ref[0] * (group ** -0.5)).astype(BF16), group, groups)
        m_s[...] = jnp.full((rows, 1), NEG, F32)
        l_s[...] = jnp.zeros((rows, 1), F32)
        acc_s[...] = jnp.zeros((rows, GROUP_WIDTH), F32)
        col = _iota((rows, t), 1)
        mask = (col < n_tok) & ((_iota((rows, t), 0) % n_tok) >= col)
        s = _dot_nt(qs_s[...], _pad_rows(kn_ref[0].astype(BF16), t))
        if fox:
            lf = -_softplus(-(sm_ref[0] + prow_ref[0:1, :]))
            logf_ref[0] = lf[:, 0:N_HEADS]
            later = (_iota((n_tok, n_tok), 0) < _iota((n_tok, n_tok), 1)).astype(F32)
            after = _dot_split_rhs(later, lf, 3)
            head_sel = (_iota((SUBLANES, LANES), 0) == _iota((SUBLANES, LANES), 1)).astype(F32)
            after_rows = _dot_split_rhs(head_sel, _pad_rows(after, t), 3, _dot_nt)
            rq = _stack_rows(after, n_tok)
            rq_s[...] = rq
            carry_s[...] = _stack_rows(jnp.broadcast_to(jnp.sum(lf, axis=0, keepdims=True), (n_tok, SMALL_W)),
                                       n_tok)
            s = s + (_stack_lanes(after_rows, n_tok) - rq)
        v_new = _pad_rows(vn_ref[0].astype(BF16), t)
        update([jnp.where(mask, s, NEG)], [lambda p: _dot(p, v_new)])

    qs = qs_s[...]
    scores = []
    products = []
    if fox:
        carry = carry_s[...]
        rq = rq_s[...]
    for j in range(pps):
        s = _dot(qs, k_refs[j][0, 0].astype(BF16))
        if fox:
            incl = _stack_lanes(r_refs[j][0], n_tok)
            excl = jnp.where(_iota((rows, t), 1) == t - 1, 0.0, pltpu.roll(incl, t - 1, axis=1))
            s = s + (excl + (carry - rq))
            carry = carry + incl[:, 0:1]
        scores.append(s)
        products.append(functools.partial(lambda p, ref: _dot_nt(p, ref[0, 0].astype(BF16)), ref=v_refs[j]))
    if fox:
        carry_s[...] = carry
    update(scores, products)

    @pl.when(step == pl.num_programs(1) - 1)
    def _finish():
        o = acc_s[...] / l_s[...]
        if fox:
            o_ref[0] = _unstack_select(o, n_tok, HEAD_DIM, range(N_HEADS))
        else:
            o1 = _unstack_select(o, n_tok, HEAD_DIM, range(N_HEADS))
            o2 = _unstack_select(o, n_tok, HEAD_DIM, range(N_HEADS, 2 * N_HEADS))
            od = o1 - _diff_lambda(lam_ref, lam_init) * o2
            ms = _dot_split(od * od, _block_ones(GROUP_WIDTH, HEAD_DIM), 2) * (1.0 / HEAD_DIM)
            o_ref[0] = od * lax.rsqrt(ms + EPS) * norm_ref[...] * (1.0 - lam_init)


def _decode_attention(page_table, q, k_new, v_new, k_pool, v_pool, layer, *, fox, extras, suffix=None,
                      lam_init=0.0):
    b, n_tok, w = q.shape
    n_pages = page_table.shape[1]
    pps = PAGES_PER_STEP
    assert n_pages % pps == 0 and n_tok % SUBLANES == 0
    steps = n_pages // pps
    tok_spec = pl.BlockSpec((1, n_tok, w), lambda i, s, pt: (i, 0, 0))
    const2 = lambda i, s, pt: (0, 0)

    def page_spec(j):
        return pl.BlockSpec((1, 1, w, PAGE_SIZE),
                            lambda i, s, pt: (layer, pt[i, n_pages - 1 - (s * pps + j)], 0, 0))

    def suffix_spec(j):
        return pl.BlockSpec((1, N_HEADS, PAGE_SIZE),
                            lambda i, s, pt: (pt[i, n_pages - 1 - (s * pps + j)], 0, 0))

    rows = (N_HEADS if fox else 2 * N_HEADS) * n_tok
    scratch = [pltpu.VMEM((rows, w), BF16), pltpu.VMEM((rows, 1), F32), pltpu.VMEM((rows, 1), F32),
               pltpu.VMEM((rows, w), F32)]
    in_specs = [tok_spec] * 3
    if fox:
        sm, prow = extras
        in_specs += [pl.BlockSpec((1, n_tok, SMALL_W), lambda i, s, pt: (i, 0, 0)),
                     pl.BlockSpec(prow.shape, const2)]
        in_specs += [page_spec(j) for j in range(pps)] * 2 + [suffix_spec(j) for j in range(pps)]
        operands = (q, k_new, v_new, sm, prow) + (k_pool,) * pps + (v_pool,) * pps + (suffix,) * pps
        out_shape = [jax.ShapeDtypeStruct((b, n_tok, w), F32), jax.ShapeDtypeStruct((b, n_tok, N_HEADS), F32)]
        out_specs = [tok_spec, pl.BlockSpec((1, n_tok, N_HEADS), lambda i, s, pt: (i, 0, 0))]
        scratch += [pltpu.VMEM((rows, 1), F32), pltpu.VMEM((rows, 1), F32)]
    else:
        lam, norm = extras
        in_specs += [pl.BlockSpec(lam.shape, const2), pl.BlockSpec(norm.shape, const2)]
        in_specs += [page_spec(j) for j in range(pps)] * 2
        operands = (q, k_new, v_new, lam, norm) + (k_pool,) * pps + (v_pool,) * pps
        out_shape = jax.ShapeDtypeStruct((b, n_tok, w), F32)
        out_specs = tok_spec
    return pl.pallas_call(
        functools.partial(_decode_body, fox=fox, n_tok=n_tok, lam_init=lam_init),
        grid_spec=pltpu.PrefetchScalarGridSpec(
            num_scalar_prefetch=1, grid=(b, steps), in_specs=in_specs, out_specs=out_specs,
            scratch_shapes=scratch),
        out_shape=out_shape,
        compiler_params=_cparams(2),
        name="decode_fox" if fox else "decode_diff",
    )(page_table, *operands)


def _split_w_in(w):
    a0, b0, c0, d0 = 0, A_COLS, A_COLS + B_COLS, A_COLS + B_COLS + C_COLS
    main = [w[:, a0:a0 + 768], w[:, b0:b0 + 768], w[:, c0:c0 + C_CONV_CH],
            w[:, c0 + C_CONV_CH:c0 + C_CONV_CH + 256], w[:, d0:d0 + 256], w[:, d0 + 256:d0 + 256 + D_CONV_CH]]
    small = jnp.concatenate([w[:, a0 + 768:a0 + 772], w[:, c0 + 1024:c0 + 1028], w[:, c0 + 1028:c0 + 1032],
                             w[:, d0 + 768:d0 + 772]], axis=1)
    small_pad = jnp.pad(small, ((0, 0), (0, SMALL_W - small.shape[1])))
    return jnp.concatenate(main + [small_pad], axis=1).astype(BF16)


def _pages_transposed(cache):
    nd = cache.ndim
    moved = cache.transpose((0, 1) + tuple(range(3, nd)) + (2,))
    return moved.reshape(cache.shape[0], cache.shape[1], -1, PAGE_SIZE)


def _gdn_state_to_block(s):
    b = s.shape[0]
    eye = jnp.eye(N_HEADS, dtype=s.dtype)
    return jnp.einsum('bhkv,hg->bhkgv', s, eye).reshape(b, GROUP_WIDTH, GROUP_WIDTH)


def _gdn_block_to_state(sb):
    b = sb.shape[0]
    s5 = sb.reshape(b, N_HEADS, HEAD_DIM, N_HEADS, HEAD_DIM)
    return jnp.stack([s5[:, h, :, h, :] for h in range(N_HEADS)], axis=1)


def _ssd_state_to_block(s):
    b = s.shape[0]
    st = s.transpose(0, 3, 1, 2)
    grp = (jnp.arange(2)[:, None] == (jnp.arange(N_HEADS) // 2)[None, :]).astype(s.dtype)
    return jnp.einsum('bnhp,gh->bgnhp', st, grp).reshape(b, 2 * D_STATE, GROUP_WIDTH)


def _ssd_block_to_state(wb):
    b = wb.shape[0]
    w5 = wb.reshape(b, 2, D_STATE, N_HEADS, HEAD_DIM)
    s = jnp.stack([w5[:, h // 2, :, h, :] for h in range(N_HEADS)], axis=1)
    return s.transpose(0, 1, 3, 2)


def _row_tile(n):
    for tm in (768, 512, 384, 256, 128):
        if n % tm == 0:
            return tm
    return n


def _trunk(x, past, weights):
    (g_mix_pre, w_in, fox_forget_bias, diff_lambda, diff_norm, gdn_conv_w, gdn_A_log, gdn_dt_bias, gdn_norm,
     ssd_conv_w, ssd_conv_b, ssd_A_log, ssd_dt_bias, ssd_D, ssd_norm, w_out, g_mix_post, g_mlp_pre,
     w_mlp_up, w_mlp_down, g_mlp_post) = weights
    bsz, seq, d = x.shape
    n = bsz * seq
    tm = _row_tile(n)
    x2 = x.reshape(n, d)
    if past is not None:
        (cache_fox_k, cache_fox_v, cache_fox_logf, cache_diff_k, cache_diff_v,
         state_gdn_conv, state_gdn, state_ssd_conv, state_ssd, page_table) = past
        fox_k_pages, fox_v_pages = _pages_transposed(cache_fox_k), _pages_transposed(cache_fox_v)
        diff_k_pages, diff_v_pages = _pages_transposed(cache_diff_k), _pages_transposed(cache_diff_v)
        logf_rows = cache_fox_logf.transpose(0, 1, 3, 2).reshape(DEPTH, -1, PAGE_SIZE)
    layer_rows = []
    for l in range(DEPTH):
        lam_init = 0.8 - 0.6 * math.exp(-0.3 * l)
        outs = _in_proj(x2, g_mix_pre[l], _split_w_in(w_in[l]), tm)
        fq, fk, fv, dq, dk, dv, cqkv, cz, sz, sxbc, sm = [o.reshape(bsz, seq, -1) for o in outs]

        bias_lanes = jnp.concatenate([fox_forget_bias[l], gdn_dt_bias[l], jnp.zeros((4,), F32), ssd_dt_bias[l]])
        prow = jnp.zeros((SUBLANES, SMALL_W), F32).at[0, 0:bias_lanes.shape[0]].set(bias_lanes)
        lam = diff_lambda[l]
        dnorm = jnp.tile(diff_norm[l], N_HEADS).reshape(1, GROUP_WIDTH)
        gdn_par = jnp.zeros((SUBLANES, GROUP_WIDTH), F32)
        gdn_par = gdn_par.at[0].set(jnp.repeat(gdn_dt_bias[l], HEAD_DIM))
        gdn_par = gdn_par.at[1].set(jnp.repeat(gdn_A_log[l], HEAD_DIM))
        gdn_par = gdn_par.at[2].set(jnp.tile(gdn_norm[l], N_HEADS))
        ssd_par = jnp.zeros((SUBLANES, GROUP_WIDTH), F32)
        ssd_par = ssd_par.at[0].set(jnp.repeat(ssd_dt_bias[l], HEAD_DIM))
        ssd_par = ssd_par.at[1].set(jnp.repeat(ssd_A_log[l], HEAD_DIM))
        ssd_par = ssd_par.at[2].set(jnp.repeat(ssd_D[l], HEAD_DIM))
        ssd_par = ssd_par.at[3].set(ssd_norm[l])
        gdn_w8 = jnp.pad(gdn_conv_w[l], ((0, SUBLANES - CONV_W), (0, 0)))
        ssd_w8 = jnp.pad(ssd_conv_w[l], ((0, SUBLANES - CONV_W), (0, 0)))
        ssd_cb = ssd_conv_b[l].reshape(1, D_CONV_CH)
        state_pad = ((0, 0), (SUBLANES - (CONV_W - 1), 0), (0, 0))

        if past is None:
            out_a, fk_t, fv_t, logf = _prompt_attention(fq, fk, fv, fox=True, extras=(sm, prow))
            out_b, dk_t, dv_t = _prompt_attention(dq, dk, dv, fox=False, extras=(lam, dnorm), lam_init=lam_init)
            from_t = lambda a, dims: jnp.moveaxis(a.reshape((bsz,) + dims + (seq,)), -1, 1)
            new_kv = (from_t(fk_t, (N_HEADS, HEAD_DIM)), from_t(fv_t, (N_HEADS, HEAD_DIM)),
                      from_t(dk_t, (N_HEADS, 2, DIFF_QK)), from_t(dv_t, (N_HEADS, HEAD_DIM)))
            gdn_buf = jnp.zeros((bsz, SUBLANES, C_CONV_CH), F32)
            gdn_s0 = jnp.zeros((bsz, GROUP_WIDTH, GROUP_WIDTH), F32)
            ssd_buf = jnp.zeros((bsz, SUBLANES, D_CONV_CH), F32)
            ssd_s0 = jnp.zeros((bsz, 2 * D_STATE, GROUP_WIDTH), F32)
        else:
            suffix = _page_suffix_sums(logf_rows, l)
            out_a, logf = _decode_attention(page_table, fq, fk, fv, fox_k_pages, fox_v_pages, l, fox=True,
                                            extras=(sm, prow), suffix=suffix)
            out_b = _decode_attention(page_table, dq, dk, dv, diff_k_pages, diff_v_pages, l, fox=False,
                                      extras=(lam, dnorm), lam_init=lam_init)
            new_kv = (fk.reshape(bsz, seq, N_HEADS, HEAD_DIM), fv.reshape(bsz, seq, N_HEADS, HEAD_DIM),
                      dk.reshape(bsz, seq, N_HEADS, 2, DIFF_QK), dv.reshape(bsz, seq, N_HEADS, HEAD_DIM))
            gdn_buf = jnp.pad(state_gdn_conv[l], state_pad)
            gdn_s0 = _gdn_state_to_block(state_gdn[l])
            ssd_buf = jnp.pad(state_ssd_conv[l], state_pad)
            ssd_s0 = _ssd_state_to_block(state_ssd[l])

        out_c, gdn_conv8, gdn_sb = _gdn(cqkv, cz, sm, gdn_buf, gdn_s0, gdn_w8, gdn_par)
        out_d, ssd_conv8, ssd_sb = _ssd(sxbc, sz, sm, ssd_buf, ssd_s0, ssd_w8, ssd_cb, ssd_par)

        mixer_outs = [o.reshape(n, GROUP_WIDTH) for o in (out_a, out_b, out_c, out_d)]
        x2 = _post(x2, mixer_outs, w_out[l].astype(BF16), g_mix_post[l], g_mlp_pre[l],
                   w_mlp_up[l].astype(BF16), w_mlp_down[l].astype(BF16), g_mlp_post[l], tm)

        tail = SUBLANES - (CONV_W - 1)
        layer_rows.append((
            new_kv[0], new_kv[1], logf, new_kv[2], new_kv[3],
            gdn_conv8[:, tail:], _gdn_block_to_state(gdn_sb),
            ssd_conv8[:, tail:], _ssd_block_to_state(ssd_sb)))
    stacked = tuple(jnp.stack([rows[i] for rows in layer_rows]) for i in range(len(layer_rows[0])))
    return x2.reshape(bsz, seq, d), stacked


def kernel(x_prompt, x_sample, cache_fox_k, cache_fox_v, cache_fox_logf, cache_diff_k, cache_diff_v, state_gdn_conv, state_gdn, state_ssd_conv, state_ssd, page_table, meta_tokens, g_mix_pre, w_in, fox_forget_bias, diff_lambda, diff_norm, gdn_conv_w, gdn_A_log, gdn_dt_bias, gdn_norm, ssd_conv_w, ssd_conv_b, ssd_A_log, ssd_dt_bias, ssd_D, ssd_norm, w_out, g_mix_post, g_mlp_pre, w_mlp_up, w_mlp_down, g_mlp_post):
    weights = (g_mix_pre, w_in, fox_forget_bias, diff_lambda, diff_norm, gdn_conv_w, gdn_A_log, gdn_dt_bias,
               gdn_norm, ssd_conv_w, ssd_conv_b, ssd_A_log, ssd_dt_bias, ssd_D, ssd_norm, w_out, g_mix_post,
               g_mlp_pre, w_mlp_up, w_mlp_down, g_mlp_post)
    bsz = x_prompt.shape[0]
    meta = jnp.broadcast_to(meta_tokens.astype(x_prompt.dtype)[None], (bsz,) + meta_tokens.shape)
    xp = jnp.concatenate([meta, x_prompt], axis=1)
    y_p, rows_p = _trunk(xp, None, weights)
    past = (cache_fox_k, cache_fox_v, cache_fox_logf, cache_diff_k, cache_diff_v,
            state_gdn_conv, state_gdn, state_ssd_conv, state_ssd, page_table)
    y_s, rows_s = _trunk(x_sample, past, weights)
    return (y_p[:, meta_tokens.shape[0]:], y_s) + rows_p + rows_s
```

```python
import functools
import math

import jax
import jax.numpy as jnp
from jax import lax
from jax.experimental import pallas as pl
from jax.experimental.pallas import tpu as pltpu

F32 = jnp.float32
BF16 = jnp.bfloat16

D_MODEL = 1024
DEPTH = 2
GROUP_WIDTH = 256
N_HEADS = 4
HEAD_DIM = 64
DIFF_QK = 32
D_STATE = 64
CONV_W = 4
C_CONV_CH = 3 * GROUP_WIDTH
D_CONV_CH = GROUP_WIDTH + 2 * 2 * D_STATE
A_COLS = 3 * GROUP_WIDTH + N_HEADS
B_COLS = 3 * GROUP_WIDTH
C_COLS = C_CONV_CH + GROUP_WIDTH + 2 * N_HEADS
D_FF = 4 * D_MODEL
PAGE_SIZE = 128
EPS = 1e-6
NEG = -1e30

LANES = 128
SUBLANES = 8
SMALL_W = 128
ATTN_Q_TILE = 256
ATTN_K_BLOCK = 256
GDN_CHUNK = 64
GDN_PRE_TILE = 256
SSD_CHUNK = 128
PAGES_PER_STEP = 32
SUFFIX_ROWS = 256
VMEM_LIMIT = 56 * 1024 * 1024

PROJ_WIDTHS = (256, 256, 256, 256, 256, 256, C_CONV_CH, 256, 256, D_CONV_CH, SMALL_W)


def _softplus(x):
    return jnp.maximum(x, 0.0) + jnp.log1p(jnp.exp(-jnp.abs(x)))


def _sigmoid(x):
    return 1.0 / (1.0 + jnp.exp(-x))


def _silu(x):
    return x * _sigmoid(x)


def _dot(a, b, prec=None):
    return jnp.dot(a, b, preferred_element_type=F32, precision=prec)


def _dot_nt(a, b, prec=None):
    return lax.dot_general(a, b, (((1,), (1,)), ((), ())), preferred_element_type=F32, precision=prec)


def _dot_tn(a, b, prec=None):
    return lax.dot_general(a, b, (((0,), (0,)), ((), ())), preferred_element_type=F32, precision=prec)


def _bf16_terms(x, terms):
    pieces = []
    r = x
    for t in range(terms):
        p = r.astype(BF16)
        pieces.append(p)
        if t + 1 < terms:
            r = r - p.astype(F32)
    return pieces


def _dot_split(x, w, terms, dot=_dot):
    w_bf = w.astype(BF16)
    acc = None
    for p in _bf16_terms(x, terms):
        part = dot(p, w_bf)
        acc = part if acc is None else acc + part
    return acc


def _dot_split_rhs(w, x, terms, dot=_dot):
    w_bf = w.astype(BF16)
    acc = None
    for p in _bf16_terms(x, terms):
        part = dot(w_bf, p)
        acc = part if acc is None else acc + part
    return acc


def _mm1(a, b):
    return _dot(a.astype(BF16), b.astype(BF16))


def _iota(shape, dim):
    return lax.broadcasted_iota(jnp.int32, shape, dim)


def _lane_group_mask(width, group, idx):
    return (_iota((1, width), 1) // group) == idx


def _block_ones(n, group):
    return ((_iota((n, n), 0) // group) == (_iota((n, n), 1) // group)).astype(F32)


def _lower_tri(n, strict=False):
    ri, ci = _iota((n, n), 0), _iota((n, n), 1)
    return (ri > ci) if strict else (ri >= ci)


def _stack_masked(x, group, groups):
    zero = jnp.zeros_like(x)
    return jnp.concatenate([jnp.where(_lane_group_mask(x.shape[1], group, g), x, zero) for g in groups], axis=0)


def _unstack_sum(xs, rows, n_groups):
    out = xs[0:rows]
    for g in range(1, n_groups):
        out = out + xs[g * rows:(g + 1) * rows]
    return out


def _unstack_select(xs, rows, group, blocks):
    out = None
    for g, blk in enumerate(blocks):
        part = jnp.where(_lane_group_mask(xs.shape[1], group, g), xs[blk * rows:(blk + 1) * rows], 0.0)
        out = part if out is None else out + part
    return out


def _pad_rows(x, rows):
    if x.shape[0] == rows:
        return x
    return jnp.concatenate([x, jnp.zeros((rows - x.shape[0], x.shape[1]), x.dtype)], axis=0)


def _cparams(n_grid_dims):
    return pltpu.CompilerParams(dimension_semantics=("arbitrary",) * n_grid_dims,
                                vmem_limit_bytes=VMEM_LIMIT)


def _in_proj_body(x_ref, g_ref, w_ref, *out_refs):
    x = x_ref[...]
    ms = jnp.mean(x * x, axis=-1, keepdims=True)
    h = (x * lax.rsqrt(ms + EPS) * g_ref[...]).astype(BF16)
    off = 0
    for o_ref, width in zip(out_refs, PROJ_WIDTHS):
        o_ref[...] = _dot(h, w_ref[:, off:off + width])
        off += width


def _in_proj(x2d, g, w_main, tm):
    n, d = x2d.shape
    assert n % tm == 0
    return pl.pallas_call(
        _in_proj_body,
        grid=(n // tm,),
        in_specs=[pl.BlockSpec((tm, d), lambda i: (i, 0)),
                  pl.BlockSpec((1, d), lambda i: (0, 0)),
                  pl.BlockSpec(w_main.shape, lambda i: (0, 0))],
        out_specs=[pl.BlockSpec((tm, w), lambda i: (i, 0)) for w in PROJ_WIDTHS],
        out_shape=[jax.ShapeDtypeStruct((n, w), F32) for w in PROJ_WIDTHS],
        compiler_params=_cparams(1),
        name="in_proj",
    )(x2d, g.reshape(1, d), w_main)


def _post_body(x_ref, oa_ref, ob_ref, oc_ref, od_ref, wo_ref, gpost_ref, gpre_ref, wu_ref, wd_ref,
               gmlp_ref, y_ref, *, ff_chunk):
    mix = None
    for i, o_ref in enumerate((oa_ref, ob_ref, oc_ref, od_ref)):
        part = _dot(o_ref[...].astype(BF16), wo_ref[i * GROUP_WIDTH:(i + 1) * GROUP_WIDTH, :])
        mix = part if mix is None else mix + part
    ms = jnp.mean(mix * mix, axis=-1, keepdims=True)
    x1 = x_ref[...] + mix * lax.rsqrt(ms + EPS) * gpost_ref[...]
    ms = jnp.mean(x1 * x1, axis=-1, keepdims=True)
    h = (x1 * lax.rsqrt(ms + EPS) * gpre_ref[...]).astype(BF16)
    acc = None
    for c in range(D_FF // ff_chunk):
        u = jnp.maximum(_dot(h, wu_ref[:, c * ff_chunk:(c + 1) * ff_chunk]), 0.0)
        part = _dot((u * u).astype(BF16), wd_ref[c * ff_chunk:(c + 1) * ff_chunk, :])
        acc = part if acc is None else acc + part
    ms = jnp.mean(acc * acc, axis=-1, keepdims=True)
    y_ref[...] = x1 + acc * lax.rsqrt(ms + EPS) * gmlp_ref[...]


def _post(x2d, outs, w_out, g_post, g_pre, w_up, w_down, g_mlp, tm):
    n, d = x2d.shape
    assert n % tm == 0
    row = lambda i: (i, 0)
    const = lambda i: (0, 0)
    single = pl.Buffered(1)
    return pl.pallas_call(
        functools.partial(_post_body, ff_chunk=512),
        grid=(n // tm,),
        in_specs=[pl.BlockSpec((tm, d), row)]
        + [pl.BlockSpec((tm, GROUP_WIDTH), row)] * 4
        + [pl.BlockSpec(w_out.shape, const, pipeline_mode=single),
           pl.BlockSpec((1, d), const), pl.BlockSpec((1, d), const),
           pl.BlockSpec(w_up.shape, const, pipeline_mode=single),
           pl.BlockSpec(w_down.shape, const, pipeline_mode=single),
           pl.BlockSpec((1, d), const)],
        out_specs=pl.BlockSpec((tm, d), row),
        out_shape=jax.ShapeDtypeStruct((n, d), F32),
        compiler_params=_cparams(1),
        name="post_mlp",
    )(x2d, *outs, w_out, g_post.reshape(1, d), g_pre.reshape(1, d), w_up, w_down, g_mlp.reshape(1, d))


def _diff_lambda(lam_ref, lam_init):
    lp = lam_ref[...]
    return (jnp.exp(jnp.sum(lp[0:1] * lp[1:2], axis=-1, keepdims=True))
            - jnp.exp(jnp.sum(lp[2:3] * lp[3:4], axis=-1, keepdims=True)) + lam_init)


def _stack_rows(x, size):
    return jnp.concatenate([x[:, h:h + 1] for h in range(N_HEADS)], axis=0)


def _stack_lanes(x, size):
    return jnp.concatenate([jnp.broadcast_to(x[h:h + 1, :], (size, x.shape[1])) for h in range(N_HEADS)],
                           axis=0)


def _prompt_attn_body(*refs, fox, seq, lam_init, n_carried):
    refs = refs[:5] + refs[5 + n_carried:]
    if fox:
        (q_ref, k_ref, v_ref, sm_ref, prow_ref, o_ref, kt_ref, vt_ref, logf_ref,
         kb_s, qt_s, vt_s, m_s, l_s, acc_s, ckx_s) = refs
        groups = list(range(N_HEADS))
        group = HEAD_DIM
    else:
        (q_ref, k_ref, v_ref, lam_ref, norm_ref, o_ref, kt_ref, vt_ref,
         kb_s, qt_s, vt_s, m_s, l_s, acc_s) = refs
        groups = [2 * h for h in range(N_HEADS)] + [2 * h + 1 for h in range(N_HEADS)]
        group = DIFF_QK
    tq, tk = ATTN_Q_TILE, ATTN_K_BLOCK
    n_q = -(-seq // tq)
    seq_pad = kb_s.shape[0]
    last_valid = seq - (n_q - 1) * tq
    n_blocks = len(groups)
    width = n_blocks * tq

    kb_s[0:seq, :] = k_ref[0].astype(BF16)
    kb_s[seq:seq_pad, :] = jnp.zeros((seq_pad - seq, GROUP_WIDTH), BF16)
    if seq_pad > n_q * tq:
        zeros = jnp.zeros((GROUP_WIDTH, seq_pad - n_q * tq), BF16)
        qt_s[:, n_q * tq:seq_pad] = zeros
        vt_s[:, n_q * tq:seq_pad] = zeros
    if fox:
        if seq_pad > n_q * tq:
            ckx_s[n_q * tq:seq_pad, :] = jnp.zeros((seq_pad - n_q * tq, N_HEADS * LANES), F32)
        bias_row = prow_ref[0:1, :]
        tri_lo = _lower_tri(tq).astype(F32)

    def stage_tile(start, valid, carry):
        qt_s[:, pl.ds(start, tq)] = (_pad_rows(q_ref[0, pl.ds(start, valid), :], tq)
                                     * (group ** -0.5)).T.astype(BF16)
        v_t = _pad_rows(v_ref[0, pl.ds(start, valid), :], tq).T
        vt_s[:, pl.ds(start, tq)] = v_t.astype(BF16)
        vt_ref[0, 0, :, pl.ds(start, valid)] = v_t[:, 0:valid]
        kt_ref[0, 0, :, pl.ds(start, valid)] = _pad_rows(k_ref[0, pl.ds(start, valid), :], tq).T[:, 0:valid]
        if not fox:
            return carry
        lf = -_softplus(-(_pad_rows(sm_ref[0, pl.ds(start, valid), :], tq) + bias_row))
        logf_ref[0, pl.ds(start, valid), :] = lf[0:valid, 0:N_HEADS]
        cc = _dot_split_rhs(tri_lo, lf, 3) + carry
        ckx_s[pl.ds(start, tq), :] = jnp.concatenate(
            [jnp.broadcast_to(cc[:, h:h + 1], (tq, LANES)) for h in range(N_HEADS)], axis=1)
        return cc[tq - 1:tq, :]

    carry = jnp.zeros((1, SMALL_W), F32)
    carry = lax.fori_loop(0, n_q - 1, lambda i, c: stage_tile(pl.multiple_of(i * tq, tq), tq, c), carry)
    stage_tile((n_q - 1) * tq, last_valid, carry)

    row_group = _iota((GROUP_WIDTH, 1), 0) // group
    lane_q = _iota((1, width), 1) % tq
    key_off = _iota((tk, 1), 0)

    def q_tile(idx, valid):
        static = isinstance(idx, int)
        qs0 = idx * tq if static else pl.multiple_of(idx * tq, tq)
        qt = qt_s[:, pl.ds(qs0, tq)]
        zero = jnp.zeros_like(qt)
        qstack = jnp.concatenate([jnp.where(row_group == g, qt, zero) for g in groups], axis=1)
        m_s[...] = jnp.full(m_s.shape, NEG, F32)
        l_s[...] = jnp.zeros(l_s.shape, F32)
        acc_s[...] = jnp.zeros(acc_s.shape, F32)

        def scores(kb0, masked):
            s = _dot(kb_s[pl.ds(kb0, tk), :], qstack)
            if fox:
                ck = ckx_s[pl.ds(kb0, tk), :]
                s = s - jnp.concatenate([ck[:, h * LANES:(h + 1) * LANES] for h in range(N_HEADS)
                                         for _ in range(tq // LANES)], axis=1)
            if masked:
                s = jnp.where((kb0 + key_off) <= (qs0 + lane_q), s, NEG)
            return s

        def absorb(s, kb0):
            m_old = m_s[...]
            m_new = jnp.maximum(m_old, jnp.max(s, axis=0, keepdims=True))
            alpha = jnp.exp(m_old - m_new)
            p = jnp.exp(s - m_new)
            l_s[...] = alpha * l_s[...] + jnp.sum(p, axis=0, keepdims=True)
            m_s[...] = m_new
            pb = p.astype(BF16)
            for b in range(n_blocks):
                h = b % N_HEADS
                upd = _dot(vt_s[h * HEAD_DIM:(h + 1) * HEAD_DIM, pl.ds(kb0, tk)], pb[:, b * tq:(b + 1) * tq])
                rows = slice(b * HEAD_DIM, (b + 1) * HEAD_DIM)
                acc_s[rows, :] = acc_s[rows, :] * alpha[:, b * tq:(b + 1) * tq] + upd

        def pair(kb_a, kb_b, mask_b):
            sa = scores(kb_a, False)
            sb = scores(kb_b, mask_b)
            absorb(sa, kb_a)
            absorb(sb, kb_b)

        def single(kb0):
            absorb(scores(kb0, True), kb0)

        n_before = (idx * tq) // tk

        def pair_step(j, c):
            kb = pl.multiple_of(2 * j * tk, 2 * tk)
            pair(kb, pl.multiple_of(kb + tk, tk), False)
            return c

        lax.fori_loop(0, n_before // 2, pair_step, 0)
        if static:
            last = n_before * tk
            if n_before % 2:
                pair(last - tk, last, True)
            else:
                single(last)
        else:
            last = pl.multiple_of(n_before * tk, tk)
            pl.when(n_before % 2 == 1)(lambda: pair(pl.multiple_of(last - tk, tk), last, True))
            pl.when(n_before % 2 == 0)(lambda: single(last))

        l = l_s[...]
        o_t = jnp.concatenate([acc_s[b * HEAD_DIM:(b + 1) * HEAD_DIM, :] / l[:, b * tq:(b + 1) * tq]
                               for b in range(n_blocks)], axis=0)
        if fox:
            o = o_t.T
        else:
            od = o_t[0:GROUP_WIDTH] - _diff_lambda(lam_ref, lam_init) * o_t[GROUP_WIDTH:]
            normed = []
            for h in range(N_HEADS):
                blk = od[h * HEAD_DIM:(h + 1) * HEAD_DIM]
                ms = jnp.mean(blk * blk, axis=0, keepdims=True)
                normed.append(blk * lax.rsqrt(ms + EPS))
            o = jnp.concatenate(normed, axis=0).T * norm_ref[...] * (1.0 - lam_init)
        o_ref[0, pl.ds(qs0, valid), :] = o[0:valid]

    def loop_body(i, c):
        q_tile(i, tq)
        return c

    lax.fori_loop(0, n_q - 1, loop_body, 0)
    q_tile(n_q - 1, last_valid)


def _prompt_attention(q, k, v, *, fox, extras, layer, carried=(), lam_init=0.0):
    b, seq, w = q.shape
    tq, tk = ATTN_Q_TILE, ATTN_K_BLOCK
    n_q = -(-seq // tq)
    seq_pad = -(-(n_q * tq) // tk) * tk
    n_blocks = N_HEADS if fox else 2 * N_HEADS
    seq_spec = pl.BlockSpec((1, seq, w), lambda i: (i, 0, 0))
    const2 = lambda i: (0, 0)
    scratch = [pltpu.VMEM((seq_pad, w), BF16), pltpu.VMEM((w, seq_pad), BF16), pltpu.VMEM((w, seq_pad), BF16),
               pltpu.VMEM((1, n_blocks * tq), F32), pltpu.VMEM((1, n_blocks * tq), F32),
               pltpu.VMEM((n_blocks * HEAD_DIM, tq), F32)]
    t_spec = pl.BlockSpec((1, 1, w, seq), lambda i: (layer, i, 0, 0))
    t_shape = jax.ShapeDtypeStruct((DEPTH, b, w, seq), F32)
    if fox:
        sm, prow = extras
        in_specs = [seq_spec] * 3 + [pl.BlockSpec((1, seq, SMALL_W), lambda i: (i, 0, 0)),
                                     pl.BlockSpec(prow.shape, const2)]
        out_shape = [jax.ShapeDtypeStruct((b, seq, w), F32)] + [t_shape] * 2 + [
            jax.ShapeDtypeStruct((b, seq, N_HEADS), F32)]
        out_specs = [seq_spec, t_spec, t_spec, pl.BlockSpec((1, seq, N_HEADS), lambda i: (i, 0, 0))]
        scratch = scratch + [pltpu.VMEM((seq_pad, N_HEADS * LANES), F32)]
    else:
        lam, norm = extras
        in_specs = [seq_spec] * 3 + [pl.BlockSpec(lam.shape, const2), pl.BlockSpec(norm.shape, const2)]
        out_shape = [jax.ShapeDtypeStruct((b, seq, w), F32)] + [t_shape] * 2
        out_specs = [seq_spec, t_spec, t_spec]
    n_inputs = len(in_specs)
    in_specs = in_specs + [pl.BlockSpec(memory_space=pl.ANY)] * len(carried)
    aliases = {n_inputs + j: 1 + j for j in range(len(carried))}
    return pl.pallas_call(
        functools.partial(_prompt_attn_body, fox=fox, seq=seq, lam_init=lam_init, n_carried=len(carried)),
        grid=(b,),
        in_specs=in_specs,
        out_specs=out_specs,
        out_shape=out_shape,
        scratch_shapes=scratch,
        input_output_aliases=aliases,
        compiler_params=_cparams(1),
        name="prompt_fox" if fox else "prompt_diff",
    )(q, k, v, *extras, *carried)


def _fill_conv_scratch(x_ref, buf_ref, xpad_s, new_ref, seq):
    xpad_s[0:SUBLANES, :] = buf_ref[0]
    xpad_s[SUBLANES:SUBLANES + seq, :] = x_ref[0]
    tail = xpad_s.shape[0] - SUBLANES - seq
    if tail:
        xpad_s[SUBLANES + seq:, :] = jnp.zeros((tail, xpad_s.shape[1]), F32)
    new_ref[0] = xpad_s[seq:seq + SUBLANES, :]


def _conv_chunk(xpad_s, w_ref, start, size):
    n = size + SUBLANES
    win = xpad_s[pl.ds(start, n), :]
    out = None
    for i in range(CONV_W):
        first = SUBLANES - (CONV_W - 1) + i
        part = pltpu.roll(win, n - first, axis=0)[0:size] * w_ref[i:i + 1, :]
        out = part if out is None else out + part
    return out


def _head_selector(first_lanes):
    parts = [(_iota((SMALL_W, GROUP_WIDTH), 0) == first + _iota((SMALL_W, GROUP_WIDTH), 1) // HEAD_DIM)
             for first in first_lanes]
    return jnp.concatenate(parts, axis=1).astype(F32)


def _expand_heads(sm, first_lane):
    return _dot_split(sm, _head_selector([first_lane]), 3)


def _head_rows(x_exp):
    sel = (_iota((SUBLANES, GROUP_WIDTH), 0) == _iota((SUBLANES, GROUP_WIDTH), 1) // HEAD_DIM)
    return _dot_split_rhs(sel.astype(F32) * (1.0 / HEAD_DIM), x_exp, 3, _dot_nt)


def _inv_unit_lower(mats, n, blk):
    ri = _iota((n, n), 0)
    ci = _iota((n, n), 1)
    eye = (ri == ci).astype(F32)
    base = min(16, blk)
    in_base = (ri // base) == (ci // base)
    ps = [-jnp.where(in_base, a, 0.0) for a in mats]
    xs = [eye + p for p in ps]
    s = 1
    while 2 * s < base:
        ps = [_mm1(p, p) for p in ps]
        xs = [x + _mm1(x, p) for x, p in zip(xs, ps)]
        s *= 2
    size = base
    while size < blk:
        off_diag = ((ri // (2 * size)) == (ci // (2 * size))) & ((ri // size) != (ci // size))
        ts = [_mm1(x, jnp.where(off_diag, a, 0.0)) for x, a in zip(xs, mats)]
        xs = [x - _mm1(t, x) for x, t in zip(xs, ts)]
        size *= 2
    return xs


def _tile_schedule(seq, tile, fn):
    n_tiles = -(-seq // tile)
    if n_tiles > 1:
        def body(i, c):
            fn(pl.multiple_of(i * tile, tile), tile)
            return c
        lax.fori_loop(0, n_tiles - 1, body, 0)
    fn((n_tiles - 1) * tile, seq - (n_tiles - 1) * tile)


def _gdn_body(x_ref, z_ref, sm_ref, buf_ref, s0_ref, w_ref, par_ref,
              o_ref, new_ref, sn_ref, xpad_s, u_s, w_s, qe_s, kd_s, pm_s, eg_s, o_s, st_s, *, seq, pre_tile):
    _fill_conv_scratch(x_ref, buf_ref, xpad_s, new_ref, seq)
    st_s[...] = s0_ref[0]
    size = GDN_CHUNK
    n = N_HEADS * size
    dtb = par_ref[0:1, :]
    a_neg = -jnp.exp(par_ref[1:2, :])
    gain = par_ref[2:3, :]
    head_ones = _block_ones(GROUP_WIDTH, HEAD_DIM)
    heads = range(N_HEADS)
    ri = _iota((n, n), 0)
    ci = _iota((n, n), 1)
    same = (ri // size) == (ci // size)
    tri = _lower_tri(size).astype(F32)

    def prepare(start, valid):
        act = _silu(_conv_chunk(xpad_s, w_ref, start, pre_tile))
        q = act[:, 0:GROUP_WIDTH]
        k = act[:, GROUP_WIDTH:2 * GROUP_WIDTH]
        v = act[:, 2 * GROUP_WIDTH:3 * GROUP_WIDTH]
        q = q * lax.rsqrt(_dot_split(q * q, head_ones, 2) + EPS) * (HEAD_DIM ** -0.5)
        k = k * lax.rsqrt(_dot_split(k * k, head_ones, 2) + EPS)
        ab = _dot_split(_pad_rows(sm_ref[0, pl.ds(start, valid), :], pre_tile), _head_selector([4, 8]), 3)
        g = a_neg * _softplus(ab[:, 0:GROUP_WIDTH] + dtb)
        beta = _sigmoid(ab[:, GROUP_WIDTH:])
        if valid < pre_tile:
            live = _iota((pre_tile, 1), 0) < valid
            g = jnp.where(live, g, 0.0)
            beta = jnp.where(live, beta, 0.0)
        first_chunk = start // size if isinstance(start, int) else lax.div(start, size)
        chunks = range(pre_tile // size)
        part = lambda x: [x[c * size:(c + 1) * size] for c in chunks]
        qc, kc, vc, bc = part(q), part(k), part(v), part(beta)
        gam = [_dot_split_rhs(tri, gi, 3) for gi in part(g)]
        egam = [jnp.exp(x) for x in gam]
        gcol = [jnp.concatenate([x[:, h * HEAD_DIM:h * HEAD_DIM + 1] for h in heads], axis=0) for x in gam]
        grow = [jnp.broadcast_to(x, (n, LANES)).T[0:1, :] for x in gcol]
        bcol = [jnp.concatenate([x[:, h * HEAD_DIM:h * HEAD_DIM + 1] for h in heads], axis=0) for x in bc]
        decay = [jnp.exp(jnp.where(same & (ri >= ci), a - b, -jnp.inf)) for a, b in zip(gcol, grow)]
        ks = [_stack_masked(x, HEAD_DIM, heads).astype(BF16) for x in kc]
        qs = [_stack_masked(x, HEAD_DIM, heads).astype(BF16) for x in qc]
        kk = [_dot_nt(x, x) for x in ks]
        a_mat = [jnp.where(same & (ri > ci), x * d, 0.0) * b for x, d, b in zip(kk, decay, bcol)]
        t_inv = _inv_unit_lower(a_mat, n, size)
        rhs = [jnp.concatenate([_stack_masked(x * b, HEAD_DIM, heads),
                                _stack_masked(y * e * b, HEAD_DIM, heads)], axis=1)
               for x, y, e, b in zip(vc, kc, egam, bc)]
        uw = [_mm1(t, r) for t, r in zip(t_inv, rhs)]
        qk = [_dot_nt(x, y) for x, y in zip(qs, ks)]
        for c in chunks:
            rows = pl.ds(start + c * size, size)
            u_s[rows, :] = _unstack_sum(uw[c][:, 0:GROUP_WIDTH], size, N_HEADS)
            w_s[rows, :] = _unstack_sum(uw[c][:, GROUP_WIDTH:], size, N_HEADS).astype(BF16)
            qe_s[rows, :] = (qc[c] * egam[c]).astype(BF16)
            g_last = gam[c][size - 1:size, :]
            kd_s[rows, :] = (kc[c] * jnp.exp(g_last - gam[c])).astype(BF16)
            idx = first_chunk + c
            pm_s[idx] = (qk[c] * decay[c]).astype(BF16)
            eg_start = idx * SUBLANES if isinstance(idx, int) else pl.multiple_of(idx * SUBLANES, SUBLANES)
            eg_s[pl.ds(eg_start, SUBLANES), :] = jnp.broadcast_to(jnp.exp(g_last), (SUBLANES, GROUP_WIDTH))

    _tile_schedule(seq, pre_tile, prepare)

    def scan(i, carry):
        rows = pl.ds(pl.multiple_of(i * size, size), size)
        state = st_s[...]
        ws = _dot(jnp.concatenate([w_s[rows, :], qe_s[rows, :]], axis=0), state.astype(BF16))
        v_new = u_s[rows, :] - ws[0:size]
        intra = _dot(pm_s[i], _stack_masked(v_new, HEAD_DIM, heads).astype(BF16))
        o_s[rows, :] = _unstack_sum(intra, size, N_HEADS) + ws[size:]
        eg = eg_s[pl.ds(pl.multiple_of(i * SUBLANES, SUBLANES), 1), :]
        st_s[...] = state * eg + _dot_tn(kd_s[rows, :], v_new.astype(BF16)) * head_ones
        return carry

    n_scanned = -(-seq // size)
    lax.fori_loop(0, n_scanned, scan, 0)
    sn_ref[0] = st_s[...]
    if o_s.shape[0] > n_scanned * size:
        o_s[n_scanned * size:, :] = jnp.zeros((o_s.shape[0] - n_scanned * size, GROUP_WIDTH), F32)

    def finish(start, valid):
        o = o_s[pl.ds(start, pre_tile), :]
        ms = _dot_split(o * o, head_ones, 2) * (1.0 / HEAD_DIM)
        y = o * lax.rsqrt(ms + EPS) * gain
        o_ref[0, pl.ds(start, valid), :] = y[0:valid] * _silu(z_ref[0, pl.ds(start, valid), :])

    _tile_schedule(seq, pre_tile, finish)


def _gdn(x, z, sm, buf8, s0, conv_w8, par):
    b, seq, ch = x.shape
    pre_tile = GDN_PRE_TILE if seq > GDN_CHUNK else GDN_CHUNK
    rows_pad = -(-seq // pre_tile) * pre_tile
    n_chunks = rows_pad // GDN_CHUNK
    bspec = lambda shape: pl.BlockSpec((1,) + shape, lambda i: (i, 0, 0))
    const2 = lambda i: (0, 0)
    return pl.pallas_call(
        functools.partial(_gdn_body, seq=seq, pre_tile=pre_tile),
        grid=(b,),
        in_specs=[bspec((seq, ch)), bspec((seq, GROUP_WIDTH)), bspec((seq, SMALL_W)), bspec((SUBLANES, ch)),
                  bspec((GROUP_WIDTH, GROUP_WIDTH)), pl.BlockSpec(conv_w8.shape, const2),
                  pl.BlockSpec(par.shape, const2)],
        out_specs=[bspec((seq, GROUP_WIDTH)), bspec((SUBLANES, ch)), bspec((GROUP_WIDTH, GROUP_WIDTH))],
        out_shape=[jax.ShapeDtypeStruct((b, seq, GROUP_WIDTH), F32),
                   jax.ShapeDtypeStruct((b, SUBLANES, ch), F32),
                   jax.ShapeDtypeStruct((b, GROUP_WIDTH, GROUP_WIDTH), F32)],
        scratch_shapes=[pltpu.VMEM((rows_pad + 2 * SUBLANES, ch), F32),
                        pltpu.VMEM((rows_pad, GROUP_WIDTH), F32),
                        pltpu.VMEM((rows_pad, GROUP_WIDTH), BF16),
                        pltpu.VMEM((rows_pad, GROUP_WIDTH), BF16),
                        pltpu.VMEM((rows_pad, GROUP_WIDTH), BF16),
                        pltpu.VMEM((n_chunks, N_HEADS * GDN_CHUNK, N_HEADS * GDN_CHUNK), BF16),
                        pltpu.VMEM((n_chunks * SUBLANES, GROUP_WIDTH), F32),
                        pltpu.VMEM((rows_pad, GROUP_WIDTH), F32),
                        pltpu.VMEM((GROUP_WIDTH, GROUP_WIDTH), F32)],
        compiler_params=_cparams(1),
        name="gdn",
    )(x, z, sm, buf8, s0, conv_w8, par)


def _ssd_body(x_ref, z_ref, sm_ref, buf_ref, s0_ref, w_ref, cb_ref, par_ref,
              o_ref, new_ref, sn_ref, xpad_s, st_s, *, seq):
    _fill_conv_scratch(x_ref, buf_ref, xpad_s, new_ref, seq)
    st_s[...] = s0_ref[0]
    size = SSD_CHUNK
    dtb = par_ref[0:1, :]
    a_neg = -jnp.exp(par_ref[1:2, :])
    d_skip = par_ref[2:3, :]
    gain = par_ref[3:4, :]
    n_state = 2 * D_STATE
    live_state = ((_iota((n_state, GROUP_WIDTH), 0) // D_STATE)
                  == (_iota((n_state, GROUP_WIDTH), 1) // (2 * HEAD_DIM))).astype(F32)

    def chunk(start, valid):
        pre = _silu(_conv_chunk(xpad_s, w_ref, start, size) + cb_ref[...])
        sx = pre[:, 0:GROUP_WIDTH]
        sb = pre[:, GROUP_WIDTH:GROUP_WIDTH + n_state]
        sc = pre[:, GROUP_WIDTH + n_state:]
        sm = _pad_rows(sm_ref[0, pl.ds(start, valid), :], size)
        dt = _softplus(_expand_heads(sm, 12) + dtb)
        if valid < size:
            dt = jnp.where(_iota((size, 1), 0) < valid, dt, 0.0)
        a = dt * a_neg
        tri = _lower_tri(size)
        cum = _dot_split_rhs(tri.astype(F32), a, 3)
        ecum = jnp.exp(cum)
        cum_rows = _head_rows(cum)
        xdt = sx * dt
        sb_bf = sb.astype(BF16)
        cb = [_dot_nt(jnp.where(_lane_group_mask(n_state, D_STATE, g), sc, 0.0).astype(BF16), sb_bf)
              for g in range(2)]
        y = None
        for h in range(N_HEADS):
            dec = jnp.exp(jnp.where(tri, cum[:, h * HEAD_DIM:h * HEAD_DIM + 1] - cum_rows[h:h + 1, :], -jnp.inf))
            xh = jnp.where(_lane_group_mask(GROUP_WIDTH, HEAD_DIM, h), xdt, 0.0).astype(BF16)
            part = _dot((cb[h // 2] * dec).astype(BF16), xh)
            y = part if y is None else y + part
        state = st_s[...]
        y = y + _dot(sc.astype(BF16), state.astype(BF16)) * ecum
        c_last = cum[size - 1:size, :]
        st_s[...] = (state * jnp.exp(c_last)
                     + _dot_tn(sb_bf, (xdt * jnp.exp(c_last - cum)).astype(BF16)) * live_state)
        ys = (y + d_skip * sx)[0:valid] * _silu(z_ref[0, pl.ds(start, valid), :])
        ms = jnp.mean(ys * ys, axis=-1, keepdims=True)
        o_ref[0, pl.ds(start, valid), :] = ys * lax.rsqrt(ms + EPS) * gain

    _tile_schedule(seq, size, chunk)
    sn_ref[0] = st_s[...]


def _ssd(x, z, sm, buf8, s0, conv_w8, conv_b, par):
    b, seq, ch = x.shape
    n_state = 2 * D_STATE
    seq_pad = -(-seq // SSD_CHUNK) * SSD_CHUNK
    bspec = lambda shape: pl.BlockSpec((1,) + shape, lambda i: (i, 0, 0))
    const2 = lambda i: (0, 0)
    return pl.pallas_call(
        functools.partial(_ssd_body, seq=seq),
        grid=(b,),
        in_specs=[bspec((seq, ch)), bspec((seq, GROUP_WIDTH)), bspec((seq, SMALL_W)), bspec((SUBLANES, ch)),
                  bspec((n_state, GROUP_WIDTH)), pl.BlockSpec(conv_w8.shape, const2),
                  pl.BlockSpec(conv_b.shape, const2), pl.BlockSpec(par.shape, const2)],
        out_specs=[bspec((seq, GROUP_WIDTH)), bspec((SUBLANES, ch)), bspec((n_state, GROUP_WIDTH))],
        out_shape=[jax.ShapeDtypeStruct((b, seq, GROUP_WIDTH), F32),
                   jax.ShapeDtypeStruct((b, SUBLANES, ch), F32),
                   jax.ShapeDtypeStruct((b, n_state, GROUP_WIDTH), F32)],
        scratch_shapes=[pltpu.VMEM((seq_pad + 2 * SUBLANES, ch), F32), pltpu.VMEM((n_state, GROUP_WIDTH), F32)],
        compiler_params=_cparams(1),
        name="ssd",
    )(x, z, sm, buf8, s0, conv_w8, conv_b, par)


def _suffix_body(x_ref, o_ref):
    later = (_iota((PAGE_SIZE, PAGE_SIZE), 0) >= _iota((PAGE_SIZE, PAGE_SIZE), 1)).astype(F32)
    o_ref[0] = _dot_split(x_ref[0], later, 3)


def _page_suffix_sums(logf_rows, layer):
    n_rows = logf_rows.shape[1]
    rows = SUFFIX_ROWS if n_rows % SUFFIX_ROWS == 0 else n_rows
    out = pl.pallas_call(
        _suffix_body,
        grid=(n_rows // rows,),
        in_specs=[pl.BlockSpec((1, rows, PAGE_SIZE), lambda i: (layer, i, 0))],
        out_specs=pl.BlockSpec((1, rows, PAGE_SIZE), lambda i: (0, i, 0)),
        out_shape=jax.ShapeDtypeStruct((1, n_rows, PAGE_SIZE), F32),
        compiler_params=_cparams(1),
        name="page_suffix",
    )(logf_rows)
    return out.reshape(n_rows // N_HEADS, N_HEADS, PAGE_SIZE)


def _decode_body(pt_ref, *refs, fox, n_tok, lam_init):
    pps = PAGES_PER_STEP
    if fox:
        q_ref, kn_ref, vn_ref, sm_ref, prow_ref = refs[:5]
        rest = refs[5:]
        k_refs, v_refs, r_refs = rest[:pps], rest[pps:2 * pps], rest[2 * pps:3 * pps]
        o_ref, logf_ref, qs_s, m_s, l_s, acc_s, rq_s, carry_s = rest[3 * pps:]
        groups = list(range(N_HEADS))
        group = HEAD_DIM
    else:
        q_ref, kn_ref, vn_ref, lam_ref, norm_ref = refs[:5]
        rest = refs[5:]
        k_refs, v_refs = rest[:pps], rest[pps:2 * pps]
        o_ref, qs_s, m_s, l_s, acc_s = rest[2 * pps:]
        groups = [2 * h for h in range(N_HEADS)] + [2 * h + 1 for h in range(N_HEADS)]
        group = DIFF_QK
    step = pl.program_id(1)
    rows = len(groups) * n_tok
    t = PAGE_SIZE

    def update(scores, value_products):
        m_old = m_s[...]
        top = scores[0]
        for s in scores[1:]:
            top = jnp.maximum(top, s)
        m_new = jnp.maximum(m_old, jnp.max(top, axis=-1, keepdims=True))
        alpha = jnp.exp(m_old - m_new)
        probs = [jnp.exp(s - m_new) for s in scores]
        total = probs[0]
        for p in probs[1:]:
            total = total + p
        l_s[...] = alpha * l_s[...] + jnp.sum(total, axis=-1, keepdims=True)
        pv = None
        for p, product in zip(probs, value_products):
            part = product(p.astype(BF16))
            pv = part if pv is None else pv + part
        acc_s[...] = alpha * acc_s[...] + pv
        m_s[...] = m_new

    @pl.when(step == 0)
    def _init():
        qs_s[...] = _stack_masked((q_ref[0] * (group ** -0.5)).astype(BF16), group, groups)
        m_s[...] = jnp.full((rows, 1), NEG, F32)
        l_s[...] = jnp.zeros((rows, 1), F32)
        acc_s[...] = jnp.zeros((rows, GROUP_WIDTH), F32)
        col = _iota((rows, t), 1)
        mask = (col < n_tok) & ((_iota((rows, t), 0) % n_tok) >= col)
        s = _dot_nt(qs_s[...], _pad_rows(kn_ref[0].astype(BF16), t))
        if fox:
            lf = -_softplus(-(sm_ref[0] + prow_ref[0:1, :]))
            logf_ref[0] = lf[:, 0:N_HEADS]
            later = (_iota((n_tok, n_tok), 0) < _iota((n_tok, n_tok), 1)).astype(F32)
            after = _dot_split_rhs(later, lf, 3)
            head_sel = (_iota((SUBLANES, LANES), 0) == _iota((SUBLANES, LANES), 1)).astype(F32)
            after_rows = _dot_split_rhs(head_sel, _pad_rows(after, t), 3, _dot_nt)
            rq = _stack_rows(after, n_tok)
            rq_s[...] = rq
            carry_s[...] = _stack_rows(jnp.broadcast_to(jnp.sum(lf, axis=0, keepdims=True), (n_tok, SMALL_W)),
                                       n_tok)
            s = s + (_stack_lanes(after_rows, n_tok) - rq)
        v_new = _pad_rows(vn_ref[0].astype(BF16), t)
        update([jnp.where(mask, s, NEG)], [lambda p: _dot(p, v_new)])

    qs = qs_s[...]
    scores = []
    products = []
    if fox:
        carry = carry_s[...]
        rq = rq_s[...]
    for j in range(pps):
        s = _dot(qs, k_refs[j][0, 0].astype(BF16))
        if fox:
            incl = _stack_lanes(r_refs[j][0], n_tok)
            excl = jnp.where(_iota((rows, t), 1) == t - 1, 0.0, pltpu.roll(incl, t - 1, axis=1))
            s = s + (excl + (carry - rq))
            carry = carry + incl[:, 0:1]
        scores.append(s)
        products.append(functools.partial(lambda p, ref: _dot_nt(p, ref[0, 0].astype(BF16)), ref=v_refs[j]))
    if fox:
        carry_s[...] = carry
    update(scores, products)

    @pl.when(step == pl.num_programs(1) - 1)
    def _finish():
        o = acc_s[...] / l_s[...]
        if fox:
            o_ref[0] = _unstack_select(o, n_tok, HEAD_DIM, range(N_HEADS))
        else:
            o1 = _unstack_select(o, n_tok, HEAD_DIM, range(N_HEADS))
            o2 = _unstack_select(o, n_tok, HEAD_DIM, range(N_HEADS, 2 * N_HEADS))
            od = o1 - _diff_lambda(lam_ref, lam_init) * o2
            ms = _dot_split(od * od, _block_ones(GROUP_WIDTH, HEAD_DIM), 2) * (1.0 / HEAD_DIM)
            o_ref[0] = od * lax.rsqrt(ms + EPS) * norm_ref[...] * (1.0 - lam_init)


def _decode_attention(page_table, q, k_new, v_new, k_pool, v_pool, layer, *, fox, extras, suffix=None,
                      lam_init=0.0):
    b, n_tok, w = q.shape
    n_pages = page_table.shape[1]
    pps = PAGES_PER_STEP
    assert n_pages % pps == 0 and n_tok % SUBLANES == 0
    steps = n_pages // pps
    tok_spec = pl.BlockSpec((1, n_tok, w), lambda i, s, pt: (i, 0, 0))
    const2 = lambda i, s, pt: (0, 0)

    def page_spec(j):
        return pl.BlockSpec((1, 1, w, PAGE_SIZE),
                            lambda i, s, pt: (layer, pt[i, n_pages - 1 - (s * pps + j)], 0, 0))

    def suffix_spec(j):
        return pl.BlockSpec((1, N_HEADS, PAGE_SIZE),
                            lambda i, s, pt: (pt[i, n_pages - 1 - (s * pps + j)], 0, 0))

    rows = (N_HEADS if fox else 2 * N_HEADS) * n_tok
    scratch = [pltpu.VMEM((rows, w), BF16), pltpu.VMEM((rows, 1), F32), pltpu.VMEM((rows, 1), F32),
               pltpu.VMEM((rows, w), F32)]
    in_specs = [tok_spec] * 3
    if fox:
        sm, prow = extras
        in_specs += [pl.BlockSpec((1, n_tok, SMALL_W), lambda i, s, pt: (i, 0, 0)),
                     pl.BlockSpec(prow.shape, const2)]
        in_specs += [page_spec(j) for j in range(pps)] * 2 + [suffix_spec(j) for j in range(pps)]
        operands = (q, k_new, v_new, sm, prow) + (k_pool,) * pps + (v_pool,) * pps + (suffix,) * pps
        out_shape = [jax.ShapeDtypeStruct((b, n_tok, w), F32), jax.ShapeDtypeStruct((b, n_tok, N_HEADS), F32)]
        out_specs = [tok_spec, pl.BlockSpec((1, n_tok, N_HEADS), lambda i, s, pt: (i, 0, 0))]
        scratch += [pltpu.VMEM((rows, 1), F32), pltpu.VMEM((rows, 1), F32)]
    else:
        lam, norm = extras
        in_specs += [pl.BlockSpec(lam.shape, const2), pl.BlockSpec(norm.shape, const2)]
        in_specs += [page_spec(j) for j in range(pps)] * 2
        operands = (q, k_new, v_new, lam, norm) + (k_pool,) * pps + (v_pool,) * pps
        out_shape = jax.ShapeDtypeStruct((b, n_tok, w), F32)
        out_specs = tok_spec
    return pl.pallas_call(
        functools.partial(_decode_body, fox=fox, n_tok=n_tok, lam_init=lam_init),
        grid_spec=pltpu.PrefetchScalarGridSpec(
            num_scalar_prefetch=1, grid=(b, steps), in_specs=in_specs, out_specs=out_specs,
            scratch_shapes=scratch),
        out_shape=out_shape,
        compiler_params=_cparams(2),
        name="decode_fox" if fox else "decode_diff",
    )(page_table, *operands)


def _split_w_in(w):
    a0, b0, c0, d0 = 0, A_COLS, A_COLS + B_COLS, A_COLS + B_COLS + C_COLS
    main = [w[:, a0:a0 + 768], w[:, b0:b0 + 768], w[:, c0:c0 + C_CONV_CH],
            w[:, c0 + C_CONV_CH:c0 + C_CONV_CH + 256], w[:, d0:d0 + 256], w[:, d0 + 256:d0 + 256 + D_CONV_CH]]
    small = jnp.concatenate([w[:, a0 + 768:a0 + 772], w[:, c0 + 1024:c0 + 1028], w[:, c0 + 1028:c0 + 1032],
                             w[:, d0 + 768:d0 + 772]], axis=1)
    small_pad = jnp.pad(small, ((0, 0), (0, SMALL_W - small.shape[1])))
    return jnp.concatenate(main + [small_pad], axis=1).astype(BF16)


def _pages_transposed(cache):
    nd = cache.ndim
    moved = cache.transpose((0, 1) + tuple(range(3, nd)) + (2,))
    return moved.reshape(cache.shape[0], cache.shape[1], -1, PAGE_SIZE)


def _gdn_state_to_block(s):
    b = s.shape[0]
    eye = jnp.eye(N_HEADS, dtype=s.dtype)
    return jnp.einsum('bhkv,hg->bhkgv', s, eye).reshape(b, GROUP_WIDTH, GROUP_WIDTH)


def _gdn_block_to_state(sb):
    b = sb.shape[0]
    s5 = sb.reshape(b, N_HEADS, HEAD_DIM, N_HEADS, HEAD_DIM)
    return jnp.stack([s5[:, h, :, h, :] for h in range(N_HEADS)], axis=1)


def _ssd_state_to_block(s):
    b = s.shape[0]
    st = s.transpose(0, 3, 1, 2)
    grp = (jnp.arange(2)[:, None] == (jnp.arange(N_HEADS) // 2)[None, :]).astype(s.dtype)
    return jnp.einsum('bnhp,gh->bgnhp', st, grp).reshape(b, 2 * D_STATE, GROUP_WIDTH)


def _ssd_block_to_state(wb):
    b = wb.shape[0]
    w5 = wb.reshape(b, 2, D_STATE, N_HEADS, HEAD_DIM)
    s = jnp.stack([w5[:, h // 2, :, h, :] for h in range(N_HEADS)], axis=1)
    return s.transpose(0, 1, 3, 2)


def _row_tile(n):
    for tm in (768, 512, 384, 256, 128):
        if n % tm == 0:
            return tm
    return n


def _trunk(x, past, weights):
    (g_mix_pre, w_in, fox_forget_bias, diff_lambda, diff_norm, gdn_conv_w, gdn_A_log, gdn_dt_bias, gdn_norm,
     ssd_conv_w, ssd_conv_b, ssd_A_log, ssd_dt_bias, ssd_D, ssd_norm, w_out, g_mix_post, g_mlp_pre,
     w_mlp_up, w_mlp_down, g_mlp_post) = weights
    bsz, seq, d = x.shape
    n = bsz * seq
    tm = _row_tile(n)
    x2 = x.reshape(n, d)
    if past is not None:
        (cache_fox_k, cache_fox_v, cache_fox_logf, cache_diff_k, cache_diff_v,
         state_gdn_conv, state_gdn, state_ssd_conv, state_ssd, page_table) = past
        fox_k_pages, fox_v_pages = _pages_transposed(cache_fox_k), _pages_transposed(cache_fox_v)
        diff_k_pages, diff_v_pages = _pages_transposed(cache_diff_k), _pages_transposed(cache_diff_v)
        logf_rows = cache_fox_logf.transpose(0, 1, 3, 2).reshape(DEPTH, -1, PAGE_SIZE)
    layer_rows = []
    fox_carried, diff_carried = (), ()
    for l in range(DEPTH):
        lam_init = 0.8 - 0.6 * math.exp(-0.3 * l)
        outs = _in_proj(x2, g_mix_pre[l], _split_w_in(w_in[l]), tm)
        fq, fk, fv, dq, dk, dv, cqkv, cz, sz, sxbc, sm = [o.reshape(bsz, seq, -1) for o in outs]

        bias_lanes = jnp.concatenate([fox_forget_bias[l], gdn_dt_bias[l], jnp.zeros((4,), F32), ssd_dt_bias[l]])
        prow = jnp.zeros((SUBLANES, SMALL_W), F32).at[0, 0:bias_lanes.shape[0]].set(bias_lanes)
        lam = diff_lambda[l]
        dnorm = jnp.tile(diff_norm[l], N_HEADS).reshape(1, GROUP_WIDTH)
        gdn_par = jnp.zeros((SUBLANES, GROUP_WIDTH), F32)
        gdn_par = gdn_par.at[0].set(jnp.repeat(gdn_dt_bias[l], HEAD_DIM))
        gdn_par = gdn_par.at[1].set(jnp.repeat(gdn_A_log[l], HEAD_DIM))
        gdn_par = gdn_par.at[2].set(jnp.tile(gdn_norm[l], N_HEADS))
        ssd_par = jnp.zeros((SUBLANES, GROUP_WIDTH), F32)
        ssd_par = ssd_par.at[0].set(jnp.repeat(ssd_dt_bias[l], HEAD_DIM))
        ssd_par = ssd_par.at[1].set(jnp.repeat(ssd_A_log[l], HEAD_DIM))
        ssd_par = ssd_par.at[2].set(jnp.repeat(ssd_D[l], HEAD_DIM))
        ssd_par = ssd_par.at[3].set(ssd_norm[l])
        gdn_w8 = jnp.pad(gdn_conv_w[l], ((0, SUBLANES - CONV_W), (0, 0)))
        ssd_w8 = jnp.pad(ssd_conv_w[l], ((0, SUBLANES - CONV_W), (0, 0)))
        ssd_cb = ssd_conv_b[l].reshape(1, D_CONV_CH)
        state_pad = ((0, 0), (SUBLANES - (CONV_W - 1), 0), (0, 0))

        if past is None:
            out_a, *fox_carried, logf = _prompt_attention(fq, fk, fv, fox=True, extras=(sm, prow), layer=l,
                                                          carried=fox_carried)
            out_b, *diff_carried = _prompt_attention(dq, dk, dv, fox=False, extras=(lam, dnorm), layer=l,
                                                     carried=diff_carried, lam_init=lam_init)
            new_kv = (None,) * 4
            gdn_buf = jnp.zeros((bsz, SUBLANES, C_CONV_CH), F32)
            gdn_s0 = jnp.zeros((bsz, GROUP_WIDTH, GROUP_WIDTH), F32)
            ssd_buf = jnp.zeros((bsz, SUBLANES, D_CONV_CH), F32)
            ssd_s0 = jnp.zeros((bsz, 2 * D_STATE, GROUP_WIDTH), F32)
        else:
            suffix = _page_suffix_sums(logf_rows, l)
            out_a, logf = _decode_attention(page_table, fq, fk, fv, fox_k_pages, fox_v_pages, l, fox=True,
                                            extras=(sm, prow), suffix=suffix)
            out_b = _decode_attention(page_table, dq, dk, dv, diff_k_pages, diff_v_pages, l, fox=False,
                                      extras=(lam, dnorm), lam_init=lam_init)
            new_kv = (fk.reshape(bsz, seq, N_HEADS, HEAD_DIM), fv.reshape(bsz, seq, N_HEADS, HEAD_DIM),
                      dk.reshape(bsz, seq, N_HEADS, 2, DIFF_QK), dv.reshape(bsz, seq, N_HEADS, HEAD_DIM))
            gdn_buf = jnp.pad(state_gdn_conv[l], state_pad)
            gdn_s0 = _gdn_state_to_block(state_gdn[l])
            ssd_buf = jnp.pad(state_ssd_conv[l], state_pad)
            ssd_s0 = _ssd_state_to_block(state_ssd[l])

        out_c, gdn_conv8, gdn_sb = _gdn(cqkv, cz, sm, gdn_buf, gdn_s0, gdn_w8, gdn_par)
        out_d, ssd_conv8, ssd_sb = _ssd(sxbc, sz, sm, ssd_buf, ssd_s0, ssd_w8, ssd_cb, ssd_par)

        mixer_outs = [o.reshape(n, GROUP_WIDTH) for o in (out_a, out_b, out_c, out_d)]
        x2 = _post(x2, mixer_outs, w_out[l].astype(BF16), g_mix_post[l], g_mlp_pre[l],
                   w_mlp_up[l].astype(BF16), w_mlp_down[l].astype(BF16), g_mlp_post[l], tm)

        tail = SUBLANES - (CONV_W - 1)
        layer_rows.append((
            new_kv[0], new_kv[1], logf, new_kv[2], new_kv[3],
            gdn_conv8[:, tail:], _gdn_block_to_state(gdn_sb),
            ssd_conv8[:, tail:], _ssd_block_to_state(ssd_sb)))
    stacked = [None if layer_rows[0][i] is None else jnp.stack([rows[i] for rows in layer_rows])
               for i in range(len(layer_rows[0]))]
    if past is None:
        from_t = lambda a, dims: jnp.moveaxis(a.reshape((DEPTH, bsz) + dims + (seq,)), -1, 2)
        stacked[0] = from_t(fox_carried[0], (N_HEADS, HEAD_DIM))
        stacked[1] = from_t(fox_carried[1], (N_HEADS, HEAD_DIM))
        stacked[3] = from_t(diff_carried[0], (N_HEADS, 2, DIFF_QK))
        stacked[4] = from_t(diff_carried[1], (N_HEADS, HEAD_DIM))
    return x2.reshape(bsz, seq, d), tuple(stacked)


def kernel(x_prompt, x_sample, cache_fox_k, cache_fox_v, cache_fox_logf, cache_diff_k, cache_diff_v, state_gdn_conv, state_gdn, state_ssd_conv, state_ssd, page_table, meta_tokens, g_mix_pre, w_in, fox_forget_bias, diff_lambda, diff_norm, gdn_conv_w, gdn_A_log, gdn_dt_bias, gdn_norm, ssd_conv_w, ssd_conv_b, ssd_A_log, ssd_dt_bias, ssd_D, ssd_norm, w_out, g_mix_post, g_mlp_pre, w_mlp_up, w_mlp_down, g_mlp_post):
    weights = (g_mix_pre, w_in, fox_forget_bias, diff_lambda, diff_norm, gdn_conv_w, gdn_A_log, gdn_dt_bias,
               gdn_norm, ssd_conv_w, ssd_conv_b, ssd_A_log, ssd_dt_bias, ssd_D, ssd_norm, w_out, g_mix_post,
               g_mlp_pre, w_mlp_up, w_mlp_down, g_mlp_post)
    bsz = x_prompt.shape[0]
    meta = jnp.broadcast_to(meta_tokens.astype(x_prompt.dtype)[None], (bsz,) + meta_tokens.shape)
    xp = jnp.concatenate([meta, x_prompt], axis=1)
    y_p, rows_p = _trunk(xp, None, weights)
    past = (cache_fox_k, cache_fox_v, cache_fox_logf, cache_diff_k, cache_diff_v,
            state_gdn_conv, state_gdn, state_ssd_conv, state_ssd, page_table)
    y_s, rows_s = _trunk(x_sample, past, weights)
    return (y_p[:, meta_tokens.shape[0]:], y_s) + rows_p + rows_s
```

```python
import functools
import math

import jax
import jax.numpy as jnp
from jax import lax
from jax.experimental import pallas as pl
from jax.experimental.pallas import tpu as pltpu

F32 = jnp.float32
BF16 = jnp.bfloat16

D_MODEL = 1024
DEPTH = 2
GROUP_WIDTH = 256
N_HEADS = 4
HEAD_DIM = 64
DIFF_QK = 32
D_STATE = 64
CONV_W = 4
C_CONV_CH = 3 * GROUP_WIDTH
D_CONV_CH = GROUP_WIDTH + 2 * 2 * D_STATE
A_COLS = 3 * GROUP_WIDTH + N_HEADS
B_COLS = 3 * GROUP_WIDTH
C_COLS = C_CONV_CH + GROUP_WIDTH + 2 * N_HEADS
D_FF = 4 * D_MODEL
PAGE_SIZE = 128
EPS = 1e-6
NEG = -1e30

LANES = 128
SUBLANES = 8
SMALL_W = 128
ATTN_Q_TILE = 256
ATTN_K_BLOCK = 256
GDN_CHUNK = 64
GDN_PRE_TILE = 256
SSD_CHUNK = 128
PAGES_PER_STEP = 32
SUFFIX_ROWS = 1024
VMEM_LIMIT = 56 * 1024 * 1024

PROJ_WIDTHS = (256, 256, 256, 256, 256, 256, C_CONV_CH, 256, 256, D_CONV_CH, SMALL_W)


def _softplus(x):
    return jnp.maximum(x, 0.0) + jnp.log1p(jnp.exp(-jnp.abs(x)))


def _sigmoid(x):
    return 1.0 / (1.0 + jnp.exp(-x))


def _silu(x):
    return x * _sigmoid(x)


def _dot(a, b, prec=None):
    return jnp.dot(a, b, preferred_element_type=F32, precision=prec)


def _dot_nt(a, b, prec=None):
    return lax.dot_general(a, b, (((1,), (1,)), ((), ())), preferred_element_type=F32, precision=prec)


def _dot_tn(a, b, prec=None):
    return lax.dot_general(a, b, (((0,), (0,)), ((), ())), preferred_element_type=F32, precision=prec)


def _bf16_terms(x, terms):
    pieces = []
    r = x
    for t in range(terms):
        p = r.astype(BF16)
        pieces.append(p)
        if t + 1 < terms:
            r = r - p.astype(F32)
    return pieces


def _dot_split(x, w, terms, dot=_dot):
    w_bf = w.astype(BF16)
    acc = None
    for p in _bf16_terms(x, terms):
        part = dot(p, w_bf)
        acc = part if acc is None else acc + part
    return acc


def _dot_split_rhs(w, x, terms, dot=_dot):
    w_bf = w.astype(BF16)
    acc = None
    for p in _bf16_terms(x, terms):
        part = dot(w_bf, p)
        acc = part if acc is None else acc + part
    return acc


def _mm1(a, b):
    return _dot(a.astype(BF16), b.astype(BF16))


def _iota(shape, dim):
    return lax.broadcasted_iota(jnp.int32, shape, dim)


def _lane_group_mask(width, group, idx):
    return (_iota((1, width), 1) // group) == idx


def _block_ones(n, group):
    return ((_iota((n, n), 0) // group) == (_iota((n, n), 1) // group)).astype(F32)


def _lower_tri(n, strict=False):
    ri, ci = _iota((n, n), 0), _iota((n, n), 1)
    return (ri > ci) if strict else (ri >= ci)


def _stack_masked(x, group, groups):
    zero = jnp.zeros_like(x)
    return jnp.concatenate([jnp.where(_lane_group_mask(x.shape[1], group, g), x, zero) for g in groups], axis=0)


def _unstack_sum(xs, rows, n_groups):
    out = xs[0:rows]
    for g in range(1, n_groups):
        out = out + xs[g * rows:(g + 1) * rows]
    return out


def _unstack_select(xs, rows, group, blocks):
    out = None
    for g, blk in enumerate(blocks):
        part = jnp.where(_lane_group_mask(xs.shape[1], group, g), xs[blk * rows:(blk + 1) * rows], 0.0)
        out = part if out is None else out + part
    return out


def _pad_rows(x, rows):
    if x.shape[0] == rows:
        return x
    return jnp.concatenate([x, jnp.zeros((rows - x.shape[0], x.shape[1]), x.dtype)], axis=0)


def _cparams(n_grid_dims):
    return pltpu.CompilerParams(dimension_semantics=("arbitrary",) * n_grid_dims,
                                vmem_limit_bytes=VMEM_LIMIT)


def _in_proj_body(x_ref, g_ref, w_ref, *out_refs):
    x = x_ref[...]
    ms = jnp.mean(x * x, axis=-1, keepdims=True)
    h = (x * lax.rsqrt(ms + EPS) * g_ref[...]).astype(BF16)
    off = 0
    for o_ref, width in zip(out_refs, PROJ_WIDTHS):
        o_ref[...] = _dot(h, w_ref[:, off:off + width])
        off += width


def _in_proj(x2d, g, w_main, tm):
    n, d = x2d.shape
    assert n % tm == 0
    return pl.pallas_call(
        _in_proj_body,
        grid=(n // tm,),
        in_specs=[pl.BlockSpec((tm, d), lambda i: (i, 0)),
                  pl.BlockSpec((1, d), lambda i: (0, 0)),
                  pl.BlockSpec(w_main.shape, lambda i: (0, 0))],
        out_specs=[pl.BlockSpec((tm, w), lambda i: (i, 0)) for w in PROJ_WIDTHS],
        out_shape=[jax.ShapeDtypeStruct((n, w), F32) for w in PROJ_WIDTHS],
        compiler_params=_cparams(1),
        name="in_proj",
    )(x2d, g.reshape(1, d), w_main)


def _post_body(x_ref, oa_ref, ob_ref, oc_ref, od_ref, wo_ref, gpost_ref, gpre_ref, wu_ref, wd_ref,
               gmlp_ref, y_ref, *, ff_chunk):
    mix = None
    for i, o_ref in enumerate((oa_ref, ob_ref, oc_ref, od_ref)):
        part = _dot(o_ref[...].astype(BF16), wo_ref[i * GROUP_WIDTH:(i + 1) * GROUP_WIDTH, :])
        mix = part if mix is None else mix + part
    ms = jnp.mean(mix * mix, axis=-1, keepdims=True)
    x1 = x_ref[...] + mix * lax.rsqrt(ms + EPS) * gpost_ref[...]
    ms = jnp.mean(x1 * x1, axis=-1, keepdims=True)
    h = (x1 * lax.rsqrt(ms + EPS) * gpre_ref[...]).astype(BF16)
    acc = None
    for c in range(D_FF // ff_chunk):
        u = jnp.maximum(_dot(h, wu_ref[:, c * ff_chunk:(c + 1) * ff_chunk]), 0.0)
        part = _dot((u * u).astype(BF16), wd_ref[c * ff_chunk:(c + 1) * ff_chunk, :])
        acc = part if acc is None else acc + part
    ms = jnp.mean(acc * acc, axis=-1, keepdims=True)
    y_ref[...] = x1 + acc * lax.rsqrt(ms + EPS) * gmlp_ref[...]


def _post(x2d, outs, w_out, g_post, g_pre, w_up, w_down, g_mlp, tm):
    n, d = x2d.shape
    assert n % tm == 0
    row = lambda i: (i, 0)
    const = lambda i: (0, 0)
    single = pl.Buffered(1)
    return pl.pallas_call(
        functools.partial(_post_body, ff_chunk=512),
        grid=(n // tm,),
        in_specs=[pl.BlockSpec((tm, d), row)]
        + [pl.BlockSpec((tm, GROUP_WIDTH), row)] * 4
        + [pl.BlockSpec(w_out.shape, const, pipeline_mode=single),
           pl.BlockSpec((1, d), const), pl.BlockSpec((1, d), const),
           pl.BlockSpec(w_up.shape, const, pipeline_mode=single),
           pl.BlockSpec(w_down.shape, const, pipeline_mode=single),
           pl.BlockSpec((1, d), const)],
        out_specs=pl.BlockSpec((tm, d), row),
        out_shape=jax.ShapeDtypeStruct((n, d), F32),
        compiler_params=_cparams(1),
        name="post_mlp",
    )(x2d, *outs, w_out, g_post.reshape(1, d), g_pre.reshape(1, d), w_up, w_down, g_mlp.reshape(1, d))


def _diff_lambda(lam_ref, lam_init):
    lp = lam_ref[...]
    return (jnp.exp(jnp.sum(lp[0:1] * lp[1:2], axis=-1, keepdims=True))
            - jnp.exp(jnp.sum(lp[2:3] * lp[3:4], axis=-1, keepdims=True)) + lam_init)


def _stack_rows(x, size):
    return jnp.concatenate([x[:, h:h + 1] for h in range(N_HEADS)], axis=0)


def _stack_lanes(x, size):
    return jnp.concatenate([jnp.broadcast_to(x[h:h + 1, :], (size, x.shape[1])) for h in range(N_HEADS)],
                           axis=0)


def _prompt_attn_body(*refs, fox, seq, lam_init, n_carried):
    refs = refs[:5] + refs[5 + n_carried:]
    if fox:
        (q_ref, k_ref, v_ref, sm_ref, prow_ref, o_ref, kt_ref, vt_ref, logf_ref,
         kb_s, qt_s, vt_s, m_s, l_s, acc_s, ckx_s) = refs
        groups = list(range(N_HEADS))
        group = HEAD_DIM
    else:
        (q_ref, k_ref, v_ref, lam_ref, norm_ref, o_ref, kt_ref, vt_ref,
         kb_s, qt_s, vt_s, m_s, l_s, acc_s) = refs
        groups = [2 * h for h in range(N_HEADS)] + [2 * h + 1 for h in range(N_HEADS)]
        group = DIFF_QK
    tq, tk = ATTN_Q_TILE, ATTN_K_BLOCK
    n_q = -(-seq // tq)
    seq_pad = kb_s.shape[0]
    last_valid = seq - (n_q - 1) * tq
    n_blocks = len(groups)
    width = n_blocks * tq

    kb_s[0:seq, :] = k_ref[0].astype(BF16)
    kb_s[seq:seq_pad, :] = jnp.zeros((seq_pad - seq, GROUP_WIDTH), BF16)
    if seq_pad > n_q * tq:
        zeros = jnp.zeros((GROUP_WIDTH, seq_pad - n_q * tq), BF16)
        qt_s[:, n_q * tq:seq_pad] = zeros
        vt_s[:, n_q * tq:seq_pad] = zeros
    if fox:
        if seq_pad > n_q * tq:
            ckx_s[n_q * tq:seq_pad, :] = jnp.zeros((seq_pad - n_q * tq, N_HEADS * LANES), F32)
        bias_row = prow_ref[0:1, :]
        tri_lo = _lower_tri(tq).astype(F32)

    def stage_tile(start, valid, carry):
        qt_s[:, pl.ds(start, tq)] = (_pad_rows(q_ref[0, pl.ds(start, valid), :], tq)
                                     * (group ** -0.5)).T.astype(BF16)
        v_t = _pad_rows(v_ref[0, pl.ds(start, valid), :], tq).T
        vt_s[:, pl.ds(start, tq)] = v_t.astype(BF16)
        vt_ref[0, 0, :, pl.ds(start, valid)] = v_t[:, 0:valid]
        kt_ref[0, 0, :, pl.ds(start, valid)] = _pad_rows(k_ref[0, pl.ds(start, valid), :], tq).T[:, 0:valid]
        if not fox:
            return carry
        lf = -_softplus(-(_pad_rows(sm_ref[0, pl.ds(start, valid), :], tq) + bias_row))
        logf_ref[0, pl.ds(start, valid), :] = lf[0:valid, 0:N_HEADS]
        cc = _dot_split_rhs(tri_lo, lf, 3) + carry
        ckx_s[pl.ds(start, tq), :] = jnp.concatenate(
            [jnp.broadcast_to(cc[:, h:h + 1], (tq, LANES)) for h in range(N_HEADS)], axis=1)
        return cc[tq - 1:tq, :]

    carry = jnp.zeros((1, SMALL_W), F32)
    carry = lax.fori_loop(0, n_q - 1, lambda i, c: stage_tile(pl.multiple_of(i * tq, tq), tq, c), carry)
    stage_tile((n_q - 1) * tq, last_valid, carry)

    row_group = _iota((GROUP_WIDTH, 1), 0) // group
    lane_q = _iota((1, width), 1) % tq
    key_off = _iota((tk, 1), 0)

    def q_tile(idx, valid):
        static = isinstance(idx, int)
        qs0 = idx * tq if static else pl.multiple_of(idx * tq, tq)
        qt = qt_s[:, pl.ds(qs0, tq)]
        zero = jnp.zeros_like(qt)
        qstack = jnp.concatenate([jnp.where(row_group == g, qt, zero) for g in groups], axis=1)
        m_s[...] = jnp.full(m_s.shape, NEG, F32)
        l_s[...] = jnp.zeros(l_s.shape, F32)
        acc_s[...] = jnp.zeros(acc_s.shape, F32)

        def scores(kb0, masked):
            s = _dot(kb_s[pl.ds(kb0, tk), :], qstack)
            if fox:
                ck = ckx_s[pl.ds(kb0, tk), :]
                s = s - jnp.concatenate([ck[:, h * LANES:(h + 1) * LANES] for h in range(N_HEADS)
                                         for _ in range(tq // LANES)], axis=1)
            if masked:
                s = jnp.where((kb0 + key_off) <= (qs0 + lane_q), s, NEG)
            return s

        def absorb(s, kb0):
            m_old = m_s[...]
            m_new = jnp.maximum(m_old, jnp.max(s, axis=0, keepdims=True))
            alpha = jnp.exp(m_old - m_new)
            p = jnp.exp(s - m_new)
            l_s[...] = alpha * l_s[...] + jnp.sum(p, axis=0, keepdims=True)
            m_s[...] = m_new
            pb = p.astype(BF16)
            for b in range(n_blocks):
                h = b % N_HEADS
                upd = _dot(vt_s[h * HEAD_DIM:(h + 1) * HEAD_DIM, pl.ds(kb0, tk)], pb[:, b * tq:(b + 1) * tq])
                rows = slice(b * HEAD_DIM, (b + 1) * HEAD_DIM)
                acc_s[rows, :] = acc_s[rows, :] * alpha[:, b * tq:(b + 1) * tq] + upd

        def pair(kb_a, kb_b, mask_b):
            sa = scores(kb_a, False)
            sb = scores(kb_b, mask_b)
            absorb(sa, kb_a)
            absorb(sb, kb_b)

        def single(kb0):
            absorb(scores(kb0, True), kb0)

        n_before = (idx * tq) // tk

        def pair_step(j, c):
            kb = pl.multiple_of(2 * j * tk, 2 * tk)
            pair(kb, pl.multiple_of(kb + tk, tk), False)
            return c

        lax.fori_loop(0, n_before // 2, pair_step, 0)
        if static:
            last = n_before * tk
            if n_before % 2:
                pair(last - tk, last, True)
            else:
                single(last)
        else:
            last = pl.multiple_of(n_before * tk, tk)
            pl.when(n_before % 2 == 1)(lambda: pair(pl.multiple_of(last - tk, tk), last, True))
            pl.when(n_before % 2 == 0)(lambda: single(last))

        l = l_s[...]
        o_t = jnp.concatenate([acc_s[b * HEAD_DIM:(b + 1) * HEAD_DIM, :] / l[:, b * tq:(b + 1) * tq]
                               for b in range(n_blocks)], axis=0)
        if fox:
            o = o_t.T
        else:
            od = o_t[0:GROUP_WIDTH] - _diff_lambda(lam_ref, lam_init) * o_t[GROUP_WIDTH:]
            normed = []
            for h in range(N_HEADS):
                blk = od[h * HEAD_DIM:(h + 1) * HEAD_DIM]
                ms = jnp.mean(blk * blk, axis=0, keepdims=True)
                normed.append(blk * lax.rsqrt(ms + EPS))
            o = jnp.concatenate(normed, axis=0).T * norm_ref[...] * (1.0 - lam_init)
        o_ref[0, pl.ds(qs0, valid), :] = o[0:valid]

    def loop_body(i, c):
        q_tile(i, tq)
        return c

    lax.fori_loop(0, n_q - 1, loop_body, 0)
    q_tile(n_q - 1, last_valid)


def _prompt_attention(q, k, v, *, fox, extras, layer, carried=(), lam_init=0.0):
    b, seq, w = q.shape
    tq, tk = ATTN_Q_TILE, ATTN_K_BLOCK
    n_q = -(-seq // tq)
    seq_pad = -(-(n_q * tq) // tk) * tk
    n_blocks = N_HEADS if fox else 2 * N_HEADS
    seq_spec = pl.BlockSpec((1, seq, w), lambda i: (i, 0, 0))
    const2 = lambda i: (0, 0)
    scratch = [pltpu.VMEM((seq_pad, w), BF16), pltpu.VMEM((w, seq_pad), BF16), pltpu.VMEM((w, seq_pad), BF16),
               pltpu.VMEM((1, n_blocks * tq), F32), pltpu.VMEM((1, n_blocks * tq), F32),
               pltpu.VMEM((n_blocks * HEAD_DIM, tq), F32)]
    t_spec = pl.BlockSpec((1, 1, w, seq), lambda i: (layer, i, 0, 0))
    t_shape = jax.ShapeDtypeStruct((DEPTH, b, w, seq), F32)
    if fox:
        sm, prow = extras
        in_specs = [seq_spec] * 3 + [pl.BlockSpec((1, seq, SMALL_W), lambda i: (i, 0, 0)),
                                     pl.BlockSpec(prow.shape, const2)]
        out_shape = [jax.ShapeDtypeStruct((b, seq, w), F32)] + [t_shape] * 2 + [
            jax.ShapeDtypeStruct((b, seq, N_HEADS), F32)]
        out_specs = [seq_spec, t_spec, t_spec, pl.BlockSpec((1, seq, N_HEADS), lambda i: (i, 0, 0))]
        scratch = scratch + [pltpu.VMEM((seq_pad, N_HEADS * LANES), F32)]
    else:
        lam, norm = extras
        in_specs = [seq_spec] * 3 + [pl.BlockSpec(lam.shape, const2), pl.BlockSpec(norm.shape, const2)]
        out_shape = [jax.ShapeDtypeStruct((b, seq, w), F32)] + [t_shape] * 2
        out_specs = [seq_spec, t_spec, t_spec]
    n_inputs = len(in_specs)
    in_specs = in_specs + [pl.BlockSpec(memory_space=pl.ANY)] * len(carried)
    aliases = {n_inputs + j: 1 + j for j in range(len(carried))}
    return pl.pallas_call(
        functools.partial(_prompt_attn_body, fox=fox, seq=seq, lam_init=lam_init, n_carried=len(carried)),
        grid=(b,),
        in_specs=in_specs,
        out_specs=out_specs,
        out_shape=out_shape,
        scratch_shapes=scratch,
        input_output_aliases=aliases,
        compiler_params=_cparams(1),
        name="prompt_fox" if fox else "prompt_diff",
    )(q, k, v, *extras, *carried)


def _fill_conv_scratch(x_ref, buf_ref, xpad_s, new_ref, seq):
    xpad_s[0:SUBLANES, :] = buf_ref[0]
    xpad_s[SUBLANES:SUBLANES + seq, :] = x_ref[0]
    tail = xpad_s.shape[0] - SUBLANES - seq
    if tail:
        xpad_s[SUBLANES + seq:, :] = jnp.zeros((tail, xpad_s.shape[1]), F32)
    new_ref[0] = xpad_s[seq:seq + SUBLANES, :]


def _conv_chunk(xpad_s, w_ref, start, size):
    n = size + SUBLANES
    win = xpad_s[pl.ds(start, n), :]
    out = None
    for i in range(CONV_W):
        first = SUBLANES - (CONV_W - 1) + i
        part = pltpu.roll(win, n - first, axis=0)[0:size] * w_ref[i:i + 1, :]
        out = part if out is None else out + part
    return out


def _head_selector(first_lanes):
    parts = [(_iota((SMALL_W, GROUP_WIDTH), 0) == first + _iota((SMALL_W, GROUP_WIDTH), 1) // HEAD_DIM)
             for first in first_lanes]
    return jnp.concatenate(parts, axis=1).astype(F32)


def _expand_heads(sm, first_lane):
    return _dot_split(sm, _head_selector([first_lane]), 3)


def _head_rows(x_exp):
    sel = (_iota((SUBLANES, GROUP_WIDTH), 0) == _iota((SUBLANES, GROUP_WIDTH), 1) // HEAD_DIM)
    return _dot_split_rhs(sel.astype(F32) * (1.0 / HEAD_DIM), x_exp, 3, _dot_nt)


def _inv_unit_lower(mats, n, blk):
    ri = _iota((n, n), 0)
    ci = _iota((n, n), 1)
    eye = (ri == ci).astype(F32)
    base = min(16, blk)
    in_base = (ri // base) == (ci // base)
    ps = [-jnp.where(in_base, a, 0.0) for a in mats]
    xs = [eye + p for p in ps]
    s = 1
    while 2 * s < base:
        ps = [_mm1(p, p) for p in ps]
        xs = [x + _mm1(x, p) for x, p in zip(xs, ps)]
        s *= 2
    size = base
    while size < blk:
        off_diag = ((ri // (2 * size)) == (ci // (2 * size))) & ((ri // size) != (ci // size))
        ts = [_mm1(x, jnp.where(off_diag, a, 0.0)) for x, a in zip(xs, mats)]
        xs = [x - _mm1(t, x) for x, t in zip(xs, ts)]
        size *= 2
    return xs


def _tile_schedule(seq, tile, fn, unroll=1):
    n_tiles = -(-seq // tile)
    if n_tiles > 1:
        def body(i, c):
            fn(pl.multiple_of(i * tile, tile), tile)
            return c
        lax.fori_loop(0, n_tiles - 1, body, 0, unroll=unroll)
    fn((n_tiles - 1) * tile, seq - (n_tiles - 1) * tile)


def _gdn_body(x_ref, z_ref, sm_ref, buf_ref, s0_ref, w_ref, par_ref,
              o_ref, new_ref, sn_ref, xpad_s, u_s, w_s, qe_s, kd_s, pm_s, eg_s, o_s, st_s, *, seq, pre_tile):
    _fill_conv_scratch(x_ref, buf_ref, xpad_s, new_ref, seq)
    st_s[...] = s0_ref[0]
    size = GDN_CHUNK
    n = N_HEADS * size
    dtb = par_ref[0:1, :]
    a_neg = -jnp.exp(par_ref[1:2, :])
    gain = par_ref[2:3, :]
    head_ones = _block_ones(GROUP_WIDTH, HEAD_DIM)
    heads = range(N_HEADS)
    ri = _iota((n, n), 0)
    ci = _iota((n, n), 1)
    same = (ri // size) == (ci // size)
    tri = _lower_tri(size).astype(F32)

    def prepare(start, valid):
        act = _silu(_conv_chunk(xpad_s, w_ref, start, pre_tile))
        q = act[:, 0:GROUP_WIDTH]
        k = act[:, GROUP_WIDTH:2 * GROUP_WIDTH]
        v = act[:, 2 * GROUP_WIDTH:3 * GROUP_WIDTH]
        q = q * lax.rsqrt(_dot_split(q * q, head_ones, 2) + EPS) * (HEAD_DIM ** -0.5)
        k = k * lax.rsqrt(_dot_split(k * k, head_ones, 2) + EPS)
        ab = _dot_split(_pad_rows(sm_ref[0, pl.ds(start, valid), :], pre_tile), _head_selector([4, 8]), 3)
        g = a_neg * _softplus(ab[:, 0:GROUP_WIDTH] + dtb)
        beta = _sigmoid(ab[:, GROUP_WIDTH:])
        if valid < pre_tile:
            live = _iota((pre_tile, 1), 0) < valid
            g = jnp.where(live, g, 0.0)
            beta = jnp.where(live, beta, 0.0)
        first_chunk = start // size if isinstance(start, int) else lax.div(start, size)
        chunks = range(pre_tile // size)
        part = lambda x: [x[c * size:(c + 1) * size] for c in chunks]
        qc, kc, vc, bc = part(q), part(k), part(v), part(beta)
        gam = [_dot_split_rhs(tri, gi, 3) for gi in part(g)]
        egam = [jnp.exp(x) for x in gam]
        gcol = [jnp.concatenate([x[:, h * HEAD_DIM:h * HEAD_DIM + 1] for h in heads], axis=0) for x in gam]
        grow = [jnp.broadcast_to(x, (n, LANES)).T[0:1, :] for x in gcol]
        bcol = [jnp.concatenate([x[:, h * HEAD_DIM:h * HEAD_DIM + 1] for h in heads], axis=0) for x in bc]
        decay = [jnp.exp(jnp.where(same & (ri >= ci), a - b, -jnp.inf)) for a, b in zip(gcol, grow)]
        ks = [_stack_masked(x, HEAD_DIM, heads).astype(BF16) for x in kc]
        qs = [_stack_masked(x, HEAD_DIM, heads).astype(BF16) for x in qc]
        kk = [_dot_nt(x, x) for x in ks]
        a_mat = [jnp.where(same & (ri > ci), x * d, 0.0) * b for x, d, b in zip(kk, decay, bcol)]
        t_inv = _inv_unit_lower(a_mat, n, size)
        rhs = [jnp.concatenate([_stack_masked(x * b, HEAD_DIM, heads),
                                _stack_masked(y * e * b, HEAD_DIM, heads)], axis=1)
               for x, y, e, b in zip(vc, kc, egam, bc)]
        uw = [_mm1(t, r) for t, r in zip(t_inv, rhs)]
        qk = [_dot_nt(x, y) for x, y in zip(qs, ks)]
        for c in chunks:
            rows = pl.ds(start + c * size, size)
            u_s[rows, :] = _unstack_sum(uw[c][:, 0:GROUP_WIDTH], size, N_HEADS)
            w_s[rows, :] = _unstack_sum(uw[c][:, GROUP_WIDTH:], size, N_HEADS).astype(BF16)
            qe_s[rows, :] = (qc[c] * egam[c]).astype(BF16)
            g_last = gam[c][size - 1:size, :]
            kd_s[rows, :] = (kc[c] * jnp.exp(g_last - gam[c])).astype(BF16)
            idx = first_chunk + c
            pm_s[idx] = (qk[c] * decay[c]).astype(BF16)
            eg_start = idx * SUBLANES if isinstance(idx, int) else pl.multiple_of(idx * SUBLANES, SUBLANES)
            eg_s[pl.ds(eg_start, SUBLANES), :] = jnp.broadcast_to(jnp.exp(g_last), (SUBLANES, GROUP_WIDTH))

    _tile_schedule(seq, pre_tile, prepare)

    def scan(i, carry):
        rows = pl.ds(pl.multiple_of(i * size, size), size)
        state = st_s[...]
        ws = _dot(jnp.concatenate([w_s[rows, :], qe_s[rows, :]], axis=0), state.astype(BF16))
        v_new = u_s[rows, :] - ws[0:size]
        intra = _dot(pm_s[i], _stack_masked(v_new, HEAD_DIM, heads).astype(BF16))
        o_s[rows, :] = _unstack_sum(intra, size, N_HEADS) + ws[size:]
        eg = eg_s[pl.ds(pl.multiple_of(i * SUBLANES, SUBLANES), 1), :]
        st_s[...] = state * eg + _dot_tn(kd_s[rows, :], v_new.astype(BF16)) * head_ones
        return carry

    n_scanned = -(-seq // size)
    lax.fori_loop(0, n_scanned, scan, 0, unroll=2)
    sn_ref[0] = st_s[...]
    if o_s.shape[0] > n_scanned * size:
        o_s[n_scanned * size:, :] = jnp.zeros((o_s.shape[0] - n_scanned * size, GROUP_WIDTH), F32)

    def finish(start, valid):
        o = o_s[pl.ds(start, pre_tile), :]
        ms = _dot_split(o * o, head_ones, 2) * (1.0 / HEAD_DIM)
        y = o * lax.rsqrt(ms + EPS) * gain
        o_ref[0, pl.ds(start, valid), :] = y[0:valid] * _silu(z_ref[0, pl.ds(start, valid), :])

    _tile_schedule(seq, pre_tile, finish)


def _gdn(x, z, sm, buf8, s0, conv_w8, par):
    b, seq, ch = x.shape
    pre_tile = GDN_PRE_TILE if seq > GDN_CHUNK else GDN_CHUNK
    rows_pad = -(-seq // pre_tile) * pre_tile
    n_chunks = rows_pad // GDN_CHUNK
    bspec = lambda shape: pl.BlockSpec((1,) + shape, lambda i: (i, 0, 0))
    const2 = lambda i: (0, 0)
    return pl.pallas_call(
        functools.partial(_gdn_body, seq=seq, pre_tile=pre_tile),
        grid=(b,),
        in_specs=[bspec((seq, ch)), bspec((seq, GROUP_WIDTH)), bspec((seq, SMALL_W)), bspec((SUBLANES, ch)),
                  bspec((GROUP_WIDTH, GROUP_WIDTH)), pl.BlockSpec(conv_w8.shape, const2),
                  pl.BlockSpec(par.shape, const2)],
        out_specs=[bspec((seq, GROUP_WIDTH)), bspec((SUBLANES, ch)), bspec((GROUP_WIDTH, GROUP_WIDTH))],
        out_shape=[jax.ShapeDtypeStruct((b, seq, GROUP_WIDTH), F32),
                   jax.ShapeDtypeStruct((b, SUBLANES, ch), F32),
                   jax.ShapeDtypeStruct((b, GROUP_WIDTH, GROUP_WIDTH), F32)],
        scratch_shapes=[pltpu.VMEM((rows_pad + 2 * SUBLANES, ch), F32),
                        pltpu.VMEM((rows_pad, GROUP_WIDTH), F32),
                        pltpu.VMEM((rows_pad, GROUP_WIDTH), BF16),
                        pltpu.VMEM((rows_pad, GROUP_WIDTH), BF16),
                        pltpu.VMEM((rows_pad, GROUP_WIDTH), BF16),
                        pltpu.VMEM((n_chunks, N_HEADS * GDN_CHUNK, N_HEADS * GDN_CHUNK), BF16),
                        pltpu.VMEM((n_chunks * SUBLANES, GROUP_WIDTH), F32),
                        pltpu.VMEM((rows_pad, GROUP_WIDTH), F32),
                        pltpu.VMEM((GROUP_WIDTH, GROUP_WIDTH), F32)],
        compiler_params=_cparams(1),
        name="gdn",
    )(x, z, sm, buf8, s0, conv_w8, par)


def _ssd_body(x_ref, z_ref, sm_ref, buf_ref, s0_ref, w_ref, cb_ref, par_ref,
              o_ref, new_ref, sn_ref, xpad_s, st_s, *, seq):
    _fill_conv_scratch(x_ref, buf_ref, xpad_s, new_ref, seq)
    st_s[...] = s0_ref[0]
    size = SSD_CHUNK
    dtb = par_ref[0:1, :]
    a_neg = -jnp.exp(par_ref[1:2, :])
    d_skip = par_ref[2:3, :]
    gain = par_ref[3:4, :]
    n_state = 2 * D_STATE
    live_state = ((_iota((n_state, GROUP_WIDTH), 0) // D_STATE)
                  == (_iota((n_state, GROUP_WIDTH), 1) // (2 * HEAD_DIM))).astype(F32)

    def chunk(start, valid):
        pre = _silu(_conv_chunk(xpad_s, w_ref, start, size) + cb_ref[...])
        sx = pre[:, 0:GROUP_WIDTH]
        sb = pre[:, GROUP_WIDTH:GROUP_WIDTH + n_state]
        sc = pre[:, GROUP_WIDTH + n_state:]
        sm = _pad_rows(sm_ref[0, pl.ds(start, valid), :], size)
        dt = _softplus(_expand_heads(sm, 12) + dtb)
        if valid < size:
            dt = jnp.where(_iota((size, 1), 0) < valid, dt, 0.0)
        a = dt * a_neg
        tri = _lower_tri(size)
        cum = _dot_split_rhs(tri.astype(F32), a, 3)
        ecum = jnp.exp(cum)
        cum_rows = _head_rows(cum)
        xdt = sx * dt
        sb_bf = sb.astype(BF16)
        cb = [_dot_nt(jnp.where(_lane_group_mask(n_state, D_STATE, g), sc, 0.0).astype(BF16), sb_bf)
              for g in range(2)]
        y = None
        for h in range(N_HEADS):
            dec = jnp.exp(jnp.where(tri, cum[:, h * HEAD_DIM:h * HEAD_DIM + 1] - cum_rows[h:h + 1, :], -jnp.inf))
            xh = jnp.where(_lane_group_mask(GROUP_WIDTH, HEAD_DIM, h), xdt, 0.0).astype(BF16)
            part = _dot((cb[h // 2] * dec).astype(BF16), xh)
            y = part if y is None else y + part
        state = st_s[...]
        y = y + _dot(sc.astype(BF16), state.astype(BF16)) * ecum
        c_last = cum[size - 1:size, :]
        st_s[...] = (state * jnp.exp(c_last)
                     + _dot_tn(sb_bf, (xdt * jnp.exp(c_last - cum)).astype(BF16)) * live_state)
        ys = (y + d_skip * sx)[0:valid] * _silu(z_ref[0, pl.ds(start, valid), :])
        ms = jnp.mean(ys * ys, axis=-1, keepdims=True)
        o_ref[0, pl.ds(start, valid), :] = ys * lax.rsqrt(ms + EPS) * gain

    _tile_schedule(seq, size, chunk, unroll=2)
    sn_ref[0] = st_s[...]


def _ssd(x, z, sm, buf8, s0, conv_w8, conv_b, par):
    b, seq, ch = x.shape
    n_state = 2 * D_STATE
    seq_pad = -(-seq // SSD_CHUNK) * SSD_CHUNK
    bspec = lambda shape: pl.BlockSpec((1,) + shape, lambda i: (i, 0, 0))
    const2 = lambda i: (0, 0)
    return pl.pallas_call(
        functools.partial(_ssd_body, seq=seq),
        grid=(b,),
        in_specs=[bspec((seq, ch)), bspec((seq, GROUP_WIDTH)), bspec((seq, SMALL_W)), bspec((SUBLANES, ch)),
                  bspec((n_state, GROUP_WIDTH)), pl.BlockSpec(conv_w8.shape, const2),
                  pl.BlockSpec(conv_b.shape, const2), pl.BlockSpec(par.shape, const2)],
        out_specs=[bspec((seq, GROUP_WIDTH)), bspec((SUBLANES, ch)), bspec((n_state, GROUP_WIDTH))],
        out_shape=[jax.ShapeDtypeStruct((b, seq, GROUP_WIDTH), F32),
                   jax.ShapeDtypeStruct((b, SUBLANES, ch), F32),
                   jax.ShapeDtypeStruct((b, n_state, GROUP_WIDTH), F32)],
        scratch_shapes=[pltpu.VMEM((seq_pad + 2 * SUBLANES, ch), F32), pltpu.VMEM((n_state, GROUP_WIDTH), F32)],
        compiler_params=_cparams(1),
        name="ssd",
    )(x, z, sm, buf8, s0, conv_w8, conv_b, par)


def _suffix_body(x_ref, o_ref):
    later = (_iota((PAGE_SIZE, PAGE_SIZE), 0) >= _iota((PAGE_SIZE, PAGE_SIZE), 1)).astype(F32)
    o_ref[0] = _dot_split(x_ref[0], later, 3)


def _page_suffix_sums(logf_rows, layer):
    n_rows = logf_rows.shape[1]
    rows = SUFFIX_ROWS if n_rows % SUFFIX_ROWS == 0 else n_rows
    out = pl.pallas_call(
        _suffix_body,
        grid=(n_rows // rows,),
        in_specs=[pl.BlockSpec((1, rows, PAGE_SIZE), lambda i: (layer, i, 0))],
        out_specs=pl.BlockSpec((1, rows, PAGE_SIZE), lambda i: (0, i, 0)),
        out_shape=jax.ShapeDtypeStruct((1, n_rows, PAGE_SIZE), F32),
        compiler_params=_cparams(1),
        name="page_suffix",
    )(logf_rows)
    return out.reshape(n_rows // N_HEADS, N_HEADS, PAGE_SIZE)


def _decode_body(pt_ref, *refs, fox, n_tok, lam_init):
    pps = PAGES_PER_STEP
    if fox:
        q_ref, kn_ref, vn_ref, sm_ref, prow_ref = refs[:5]
        rest = refs[5:]
        k_refs, v_refs, r_refs = rest[:pps], rest[pps:2 * pps], rest[2 * pps:3 * pps]
        o_ref, logf_ref, qs_s, m_s, l_s, acc_s, rq_s, carry_s = rest[3 * pps:]
        groups = list(range(N_HEADS))
        group = HEAD_DIM
    else:
        q_ref, kn_ref, vn_ref, lam_ref, norm_ref = refs[:5]
        rest = refs[5:]
        k_refs, v_refs = rest[:pps], rest[pps:2 * pps]
        o_ref, qs_s, m_s, l_s, acc_s = rest[2 * pps:]
        groups = [2 * h for h in range(N_HEADS)] + [2 * h + 1 for h in range(N_HEADS)]
        group = DIFF_QK
    step = pl.program_id(1)
    rows = len(groups) * n_tok
    t = PAGE_SIZE

    def update(scores, value_products):
        m_old = m_s[...]
        top = scores[0]
        for s in scores[1:]:
            top = jnp.maximum(top, s)
        m_new = jnp.maximum(m_old, jnp.max(top, axis=-1, keepdims=True))
        alpha = jnp.exp(m_old - m_new)
        probs = [jnp.exp(s - m_new) for s in scores]
        total = probs[0]
        for p in probs[1:]:
            total = total + p
        l_s[...] = alpha * l_s[...] + jnp.sum(total, axis=-1, keepdims=True)
        pv = None
        for p, product in zip(probs, value_products):
            part = product(p.astype(BF16))
            pv = part if pv is None else pv + part
        acc_s[...] = alpha * acc_s[...] + pv
        m_s[...] = m_new

    @pl.when(step == 0)
    def _init():
        qs_s[...] = _stack_masked((q_ref[0] * (group ** -0.5)).astype(BF16), group, groups)
        m_s[...] = jnp.full((rows, 1), NEG, F32)
        l_s[...] = jnp.zeros((rows, 1), F32)
        acc_s[...] = jnp.zeros((rows, GROUP_WIDTH), F32)
        col = _iota((rows, t), 1)
        mask = (col < n_tok) & ((_iota((rows, t), 0) % n_tok) >= col)
        s = _dot_nt(qs_s[...], _pad_rows(kn_ref[0].astype(BF16), t))
        if fox:
            lf = -_softplus(-(sm_ref[0] + prow_ref[0:1, :]))
            logf_ref[0] = lf[:, 0:N_HEADS]
            later = (_iota((n_tok, n_tok), 0) < _iota((n_tok, n_tok), 1)).astype(F32)
            after = _dot_split_rhs(later, lf, 3)
            head_sel = (_iota((SUBLANES, LANES), 0) == _iota((SUBLANES, LANES), 1)).astype(F32)
            after_rows = _dot_split_rhs(head_sel, _pad_rows(after, t), 3, _dot_nt)
            rq = _stack_rows(after, n_tok)
            rq_s[...] = rq
            carry_s[...] = _stack_rows(jnp.broadcast_to(jnp.sum(lf, axis=0, keepdims=True), (n_tok, SMALL_W)),
                                       n_tok)
            s = s + (_stack_lanes(after_rows, n_tok) - rq)
        v_new = _pad_rows(vn_ref[0].astype(BF16), t)
        update([jnp.where(mask, s, NEG)], [lambda p: _dot(p, v_new)])

    qs = qs_s[...]
    scores = []
    products = []
    if fox:
        carry = carry_s[...]
        rq = rq_s[...]
    for j in range(pps):
        s = _dot(qs, k_refs[j][0, 0].astype(BF16))
        if fox:
            incl = _stack_lanes(r_refs[j][0], n_tok)
            excl = jnp.where(_iota((rows, t), 1) == t - 1, 0.0, pltpu.roll(incl, t - 1, axis=1))
            s = s + (excl + (carry - rq))
            carry = carry + incl[:, 0:1]
        scores.append(s)
        products.append(functools.partial(lambda p, ref: _dot_nt(p, ref[0, 0].astype(BF16)), ref=v_refs[j]))
    if fox:
        carry_s[...] = carry
    update(scores, products)

    @pl.when(step == pl.num_programs(1) - 1)
    def _finish():
        o = acc_s[...] / l_s[...]
        if fox:
            o_ref[0] = _unstack_select(o, n_tok, HEAD_DIM, range(N_HEADS))
        else:
            o1 = _unstack_select(o, n_tok, HEAD_DIM, range(N_HEADS))
            o2 = _unstack_select(o, n_tok, HEAD_DIM, range(N_HEADS, 2 * N_HEADS))
            od = o1 - _diff_lambda(lam_ref, lam_init) * o2
            ms = _dot_split(od * od, _block_ones(GROUP_WIDTH, HEAD_DIM), 2) * (1.0 / HEAD_DIM)
            o_ref[0] = od * lax.rsqrt(ms + EPS) * norm_ref[...] * (1.0 - lam_init)


def _decode_attention(page_table, q, k_new, v_new, k_pool, v_pool, layer, *, fox, extras, suffix=None,
                      lam_init=0.0):
    b, n_tok, w = q.shape
    n_pages = page_table.shape[1]
    pps = PAGES_PER_STEP
    assert n_pages % pps == 0 and n_tok % SUBLANES == 0
    steps = n_pages // pps
    tok_spec = pl.BlockSpec((1, n_tok, w), lambda i, s, pt: (i, 0, 0))
    const2 = lambda i, s, pt: (0, 0)

    def page_spec(j):
        return pl.BlockSpec((1, 1, w, PAGE_SIZE),
                            lambda i, s, pt: (layer, pt[i, n_pages - 1 - (s * pps + j)], 0, 0))

    def suffix_spec(j):
        return pl.BlockSpec((1, N_HEADS, PAGE_SIZE),
                            lambda i, s, pt: (pt[i, n_pages - 1 - (s * pps + j)], 0, 0))

    rows = (N_HEADS if fox else 2 * N_HEADS) * n_tok
    scratch = [pltpu.VMEM((rows, w), BF16), pltpu.VMEM((rows, 1), F32), pltpu.VMEM((rows, 1), F32),
               pltpu.VMEM((rows, w), F32)]
    in_specs = [tok_spec] * 3
    if fox:
        sm, prow = extras
        in_specs += [pl.BlockSpec((1, n_tok, SMALL_W), lambda i, s, pt: (i, 0, 0)),
                     pl.BlockSpec(prow.shape, const2)]
        in_specs += [page_spec(j) for j in range(pps)] * 2 + [suffix_spec(j) for j in range(pps)]
        operands = (q, k_new, v_new, sm, prow) + (k_pool,) * pps + (v_pool,) * pps + (suffix,) * pps
        out_shape = [jax.ShapeDtypeStruct((b, n_tok, w), F32), jax.ShapeDtypeStruct((b, n_tok, N_HEADS), F32)]
        out_specs = [tok_spec, pl.BlockSpec((1, n_tok, N_HEADS), lambda i, s, pt: (i, 0, 0))]
        scratch += [pltpu.VMEM((rows, 1), F32), pltpu.VMEM((rows, 1), F32)]
    else:
        lam, norm = extras
        in_specs += [pl.BlockSpec(lam.shape, const2), pl.BlockSpec(norm.shape, const2)]
        in_specs += [page_spec(j) for j in range(pps)] * 2
        operands = (q, k_new, v_new, lam, norm) + (k_pool,) * pps + (v_pool,) * pps
        out_shape = jax.ShapeDtypeStruct((b, n_tok, w), F32)
        out_specs = tok_spec
    return pl.pallas_call(
        functools.partial(_decode_body, fox=fox, n_tok=n_tok, lam_init=lam_init),
        grid_spec=pltpu.PrefetchScalarGridSpec(
            num_scalar_prefetch=1, grid=(b, steps), in_specs=in_specs, out_specs=out_specs,
            scratch_shapes=scratch),
        out_shape=out_shape,
        compiler_params=_cparams(2),
        name="decode_fox" if fox else "decode_diff",
    )(page_table, *operands)


def _split_w_in(w):
    a0, b0, c0, d0 = 0, A_COLS, A_COLS + B_COLS, A_COLS + B_COLS + C_COLS
    main = [w[:, a0:a0 + 768], w[:, b0:b0 + 768], w[:, c0:c0 + C_CONV_CH],
            w[:, c0 + C_CONV_CH:c0 + C_CONV_CH + 256], w[:, d0:d0 + 256], w[:, d0 + 256:d0 + 256 + D_CONV_CH]]
    small = jnp.concatenate([w[:, a0 + 768:a0 + 772], w[:, c0 + 1024:c0 + 1028], w[:, c0 + 1028:c0 + 1032],
                             w[:, d0 + 768:d0 + 772]], axis=1)
    small_pad = jnp.pad(small, ((0, 0), (0, SMALL_W - small.shape[1])))
    return jnp.concatenate(main + [small_pad], axis=1).astype(BF16)


def _pages_transposed(cache):
    nd = cache.ndim
    moved = cache.transpose((0, 1) + tuple(range(3, nd)) + (2,))
    return moved.reshape(cache.shape[0], cache.shape[1], -1, PAGE_SIZE)


def _gdn_state_to_block(s):
    b = s.shape[0]
    eye = jnp.eye(N_HEADS, dtype=s.dtype)
    return jnp.einsum('bhkv,hg->bhkgv', s, eye).reshape(b, GROUP_WIDTH, GROUP_WIDTH)


def _gdn_block_to_state(sb):
    b = sb.shape[0]
    s5 = sb.reshape(b, N_HEADS, HEAD_DIM, N_HEADS, HEAD_DIM)
    return jnp.stack([s5[:, h, :, h, :] for h in range(N_HEADS)], axis=1)


def _ssd_state_to_block(s):
    b = s.shape[0]
    st = s.transpose(0, 3, 1, 2)
    grp = (jnp.arange(2)[:, None] == (jnp.arange(N_HEADS) // 2)[None, :]).astype(s.dtype)
    return jnp.einsum('bnhp,gh->bgnhp', st, grp).reshape(b, 2 * D_STATE, GROUP_WIDTH)


def _ssd_block_to_state(wb):
    b = wb.shape[0]
    w5 = wb.reshape(b, 2, D_STATE, N_HEADS, HEAD_DIM)
    s = jnp.stack([w5[:, h // 2, :, h, :] for h in range(N_HEADS)], axis=1)
    return s.transpose(0, 1, 3, 2)


def _row_tile(n):
    for tm in (768, 512, 384, 256, 128):
        if n % tm == 0:
            return tm
    return n


def _trunk(x, past, weights):
    (g_mix_pre, w_in, fox_forget_bias, diff_lambda, diff_norm, gdn_conv_w, gdn_A_log, gdn_dt_bias, gdn_norm,
     ssd_conv_w, ssd_conv_b, ssd_A_log, ssd_dt_bias, ssd_D, ssd_norm, w_out, g_mix_post, g_mlp_pre,
     w_mlp_up, w_mlp_down, g_mlp_post) = weights
    bsz, seq, d = x.shape
    n = bsz * seq
    tm = _row_tile(n)
    x2 = x.reshape(n, d)
    if past is not None:
        (cache_fox_k, cache_fox_v, cache_fox_logf, cache_diff_k, cache_diff_v,
         state_gdn_conv, state_gdn, state_ssd_conv, state_ssd, page_table) = past
        fox_k_pages, fox_v_pages = _pages_transposed(cache_fox_k), _pages_transposed(cache_fox_v)
        diff_k_pages, diff_v_pages = _pages_transposed(cache_diff_k), _pages_transposed(cache_diff_v)
        logf_rows = cache_fox_logf.transpose(0, 1, 3, 2).reshape(DEPTH, -1, PAGE_SIZE)
    layer_rows = []
    fox_carried, diff_carried = (), ()
    for l in range(DEPTH):
        lam_init = 0.8 - 0.6 * math.exp(-0.3 * l)
        outs = _in_proj(x2, g_mix_pre[l], _split_w_in(w_in[l]), tm)
        fq, fk, fv, dq, dk, dv, cqkv, cz, sz, sxbc, sm = [o.reshape(bsz, seq, -1) for o in outs]

        bias_lanes = jnp.concatenate([fox_forget_bias[l], gdn_dt_bias[l], jnp.zeros((4,), F32), ssd_dt_bias[l]])
        prow = jnp.zeros((SUBLANES, SMALL_W), F32).at[0, 0:bias_lanes.shape[0]].set(bias_lanes)
        lam = diff_lambda[l]
        dnorm = jnp.tile(diff_norm[l], N_HEADS).reshape(1, GROUP_WIDTH)
        gdn_par = jnp.zeros((SUBLANES, GROUP_WIDTH), F32)
        gdn_par = gdn_par.at[0].set(jnp.repeat(gdn_dt_bias[l], HEAD_DIM))
        gdn_par = gdn_par.at[1].set(jnp.repeat(gdn_A_log[l], HEAD_DIM))
        gdn_par = gdn_par.at[2].set(jnp.tile(gdn_norm[l], N_HEADS))
        ssd_par = jnp.zeros((SUBLANES, GROUP_WIDTH), F32)
        ssd_par = ssd_par.at[0].set(jnp.repeat(ssd_dt_bias[l], HEAD_DIM))
        ssd_par = ssd_par.at[1].set(jnp.repeat(ssd_A_log[l], HEAD_DIM))
        ssd_par = ssd_par.at[2].set(jnp.repeat(ssd_D[l], HEAD_DIM))
        ssd_par = ssd_par.at[3].set(ssd_norm[l])
        gdn_w8 = jnp.pad(gdn_conv_w[l], ((0, SUBLANES - CONV_W), (0, 0)))
        ssd_w8 = jnp.pad(ssd_conv_w[l], ((0, SUBLANES - CONV_W), (0, 0)))
        ssd_cb = ssd_conv_b[l].reshape(1, D_CONV_CH)
        state_pad = ((0, 0), (SUBLANES - (CONV_W - 1), 0), (0, 0))

        if past is None:
            out_a, *fox_carried, logf = _prompt_attention(fq, fk, fv, fox=True, extras=(sm, prow), layer=l,
                                                          carried=fox_carried)
            out_b, *diff_carried = _prompt_attention(dq, dk, dv, fox=False, extras=(lam, dnorm), layer=l,
                                                     carried=diff_carried, lam_init=lam_init)
            new_kv = (None,) * 4
            gdn_buf = jnp.zeros((bsz, SUBLANES, C_CONV_CH), F32)
            gdn_s0 = jnp.zeros((bsz, GROUP_WIDTH, GROUP_WIDTH), F32)
            ssd_buf = jnp.zeros((bsz, SUBLANES, D_CONV_CH), F32)
            ssd_s0 = jnp.zeros((bsz, 2 * D_STATE, GROUP_WIDTH), F32)
        else:
            suffix = _page_suffix_sums(logf_rows, l)
            out_a, logf = _decode_attention(page_table, fq, fk, fv, fox_k_pages, fox_v_pages, l, fox=True,
                                            extras=(sm, prow), suffix=suffix)
            out_b = _decode_attention(page_table, dq, dk, dv, diff_k_pages, diff_v_pages, l, fox=False,
                                      extras=(lam, dnorm), lam_init=lam_init)
            new_kv = (fk.reshape(bsz, seq, N_HEADS, HEAD_DIM), fv.reshape(bsz, seq, N_HEADS, HEAD_DIM),
                      dk.reshape(bsz, seq, N_HEADS, 2, DIFF_QK), dv.reshape(bsz, seq, N_HEADS, HEAD_DIM))
            gdn_buf = jnp.pad(state_gdn_conv[l], state_pad)
            gdn_s0 = _gdn_state_to_block(state_gdn[l])
            ssd_buf = jnp.pad(state_ssd_conv[l], state_pad)
            ssd_s0 = _ssd_state_to_block(state_ssd[l])

        out_c, gdn_conv8, gdn_sb = _gdn(cqkv, cz, sm, gdn_buf, gdn_s0, gdn_w8, gdn_par)
        out_d, ssd_conv8, ssd_sb = _ssd(sxbc, sz, sm, ssd_buf, ssd_s0, ssd_w8, ssd_cb, ssd_par)

        mixer_outs = [o.reshape(n, GROUP_WIDTH) for o in (out_a, out_b, out_c, out_d)]
        x2 = _post(x2, mixer_outs, w_out[l].astype(BF16), g_mix_post[l], g_mlp_pre[l],
                   w_mlp_up[l].astype(BF16), w_mlp_down[l].astype(BF16), g_mlp_post[l], tm)

        tail = SUBLANES - (CONV_W - 1)
        layer_rows.append((
            new_kv[0], new_kv[1], logf, new_kv[2], new_kv[3],
            gdn_conv8[:, tail:], _gdn_block_to_state(gdn_sb),
            ssd_conv8[:, tail:], _ssd_block_to_state(ssd_sb)))
    stacked = [None if layer_rows[0][i] is None else jnp.stack([rows[i] for rows in layer_rows])
               for i in range(len(layer_rows[0]))]
    if past is None:
        from_t = lambda a, dims: jnp.moveaxis(a.reshape((DEPTH, bsz) + dims + (seq,)), -1, 2)
        stacked[0] = from_t(fox_carried[0], (N_HEADS, HEAD_DIM))
        stacked[1] = from_t(fox_carried[1], (N_HEADS, HEAD_DIM))
        stacked[3] = from_t(diff_carried[0], (N_HEADS, 2, DIFF_QK))
        stacked[4] = from_t(diff_carried[1], (N_HEADS, HEAD_DIM))
    return x2.reshape(bsz, seq, d), tuple(stacked)


def kernel(x_prompt, x_sample, cache_fox_k, cache_fox_v, cache_fox_logf, cache_diff_k, cache_diff_v, state_gdn_conv, state_gdn, state_ssd_conv, state_ssd, page_table, meta_tokens, g_mix_pre, w_in, fox_forget_bias, diff_lambda, diff_norm, gdn_conv_w, gdn_A_log, gdn_dt_bias, gdn_norm, ssd_conv_w, ssd_conv_b, ssd_A_log, ssd_dt_bias, ssd_D, ssd_norm, w_out, g_mix_post, g_mlp_pre, w_mlp_up, w_mlp_down, g_mlp_post):
    weights = (g_mix_pre, w_in, fox_forget_bias, diff_lambda, diff_norm, gdn_conv_w, gdn_A_log, gdn_dt_bias,
               gdn_norm, ssd_conv_w, ssd_conv_b, ssd_A_log, ssd_dt_bias, ssd_D, ssd_norm, w_out, g_mix_post,
               g_mlp_pre, w_mlp_up, w_mlp_down, g_mlp_post)
    bsz = x_prompt.shape[0]
    meta = jnp.broadcast_to(meta_tokens.astype(x_prompt.dtype)[None], (bsz,) + meta_tokens.shape)
    xp = jnp.concatenate([meta, x_prompt], axis=1)
    y_p, rows_p = _trunk(xp, None, weights)
    past = (cache_fox_k, cache_fox_v, cache_fox_logf, cache_diff_k, cache_diff_v,
            state_gdn_conv, state_gdn, state_ssd_conv, state_ssd, page_table)
    y_s, rows_s = _trunk(x_sample, past, weights)
    return (y_p[:, meta_tokens.shape[0]:], y_s) + rows_p + rows_s
```
